```python
import jax
import jax.numpy as jnp
from jax import lax
import numpy as np

D_MODEL = 1024
BATCH = 8
SEQ = 4096
DEPTH = 4

N_MIXERS = 3
CHUNK = 128
Q_BLOCK = 128
NORM_EPS = 1e-6
MEM_TOKENS = 256
MEM_HEADS = 4
MEM_HEAD_DIM = 128
MEM_WIDTH = MEM_HEADS * MEM_HEAD_DIM
GM_GROUPS = 8
GM_GROUP_DIM = 128
GM_WIDTH = GM_GROUPS * GM_GROUP_DIM
MLA_HEADS = 8
MLA_NOPE = 128
MLA_ROPE = 64
MLA_V = 128
MLA_Q_LORA = 768
MLA_KV_LORA = 256
ROPE_THETA = 10000.0
ML_HEADS = 4
ML_DV = 256
ML_DK = 128
CONV_K = 4
FF_DENSE = 2816
N_EXPERTS = 8
TOP_K = 2
FF_EXPERT = 1792

kernel_name = 'hybrid_gmlp_mla_mlstm_memory_moe_trunk'


def rms_norm(x, g):
    xf = x.astype(jnp.float32)
    y = xf * lax.rsqrt(jnp.mean(xf * xf, axis=-1, keepdims=True) + NORM_EPS)
    return (y * g.astype(jnp.float32)).astype(x.dtype)


def swiglu(x, w_gate, w_up, w_down):
    return (jax.nn.silu(x @ w_gate) * (x @ w_up)) @ w_down


def rope_angles(positions, dim):
    inv_freq = 1.0 / (ROPE_THETA ** (jnp.arange(0, dim, 2, dtype=jnp.float32) / dim))
    ang = positions.astype(jnp.float32)[..., None] * inv_freq
    return jnp.cos(ang), jnp.sin(ang)


def apply_rope(x, cos, sin):
    xf = x.astype(jnp.float32)
    x1, x2 = jnp.split(xf, 2, axis=-1)
    c = cos[:, :, None, :]
    s = sin[:, :, None, :]
    return jnp.concatenate([x1 * c - x2 * s, x2 * c + x1 * s], axis=-1).astype(x.dtype)


def memory_cross_attention(q_cols, mem, mem_norm_g, w_mem_kv):
    b, s, _ = q_cols.shape
    q = q_cols.reshape(b, s, MEM_HEADS, MEM_HEAD_DIM)
    kv = rms_norm(mem, mem_norm_g) @ w_mem_kv
    k, v = jnp.split(kv, 2, axis=-1)
    k = k.reshape(b, -1, MEM_HEADS, MEM_HEAD_DIM)
    v = v.reshape(b, -1, MEM_HEADS, MEM_HEAD_DIM)
    scores = jnp.einsum('bshd,bmhd->bhsm', q, k).astype(jnp.float32) * MEM_HEAD_DIM ** -0.5
    p = jax.nn.softmax(scores, axis=-1).astype(v.dtype)
    return jnp.einsum('bhsm,bmhd->bshd', p, v).reshape(b, s, MEM_WIDTH)


def gmlp_mixer(cols, v_norm_g, w_s, b_s):
    b, s, _ = cols.shape
    u, v = jnp.split(jax.nn.gelu(cols, approximate=False), 2, axis=-1)
    n = s // CHUNK
    v = rms_norm(v.reshape(b, n, CHUNK, GM_GROUPS, GM_GROUP_DIM),
                 v_norm_g.reshape(GM_GROUPS, GM_GROUP_DIM))
    causal = jnp.tril(jnp.ones((CHUNK, CHUNK), dtype=w_s.dtype))
    mixed = jnp.einsum('gts,bnsgc->bntgc', w_s * causal, v) + b_s.T[:, :, None]
    return u * mixed.reshape(b, s, GM_WIDTH)


def mla_mixer(cols, positions, q_norm_g, w_uq, kv_norm_g, w_ukv):
    b, s, _ = cols.shape
    c_q = cols[..., :MLA_Q_LORA]
    c_kv = cols[..., MLA_Q_LORA:MLA_Q_LORA + MLA_KV_LORA]
    k_rope = cols[..., MLA_Q_LORA + MLA_KV_LORA:]
    q = (rms_norm(c_q, q_norm_g) @ w_uq).reshape(b, s, MLA_HEADS, MLA_NOPE + MLA_ROPE)
    kv = (rms_norm(c_kv, kv_norm_g) @ w_ukv).reshape(b, s, MLA_HEADS, MLA_NOPE + MLA_V)
    q_nope, q_rope = q[..., :MLA_NOPE], q[..., MLA_NOPE:]
    k_nope, v = kv[..., :MLA_NOPE], kv[..., MLA_NOPE:]
    cos, sin = rope_angles(positions, MLA_ROPE)
    q_rope = apply_rope(q_rope, cos, sin)
    k_rope = apply_rope(k_rope[:, :, None, :], cos, sin)[:, :, 0, :]
    scale = (MLA_NOPE + MLA_ROPE) ** -0.5
    nb = s // Q_BLOCK

    def to_blocks(t):
        return jnp.moveaxis(t.reshape(b, nb, Q_BLOCK, *t.shape[2:]), 1, 0)

    key_pos = jnp.arange(s)

    def attend_block(args):
        qn, qr, blk = args
        scores = (jnp.einsum('bqhd,bkhd->bhqk', qn, k_nope)
                  + jnp.einsum('bqhr,bkr->bhqk', qr, k_rope)).astype(jnp.float32) * scale
        query_pos = blk * Q_BLOCK + jnp.arange(Q_BLOCK)
        scores = jnp.where(key_pos[None, :] <= query_pos[:, None], scores, -jnp.inf)
        p = jax.nn.softmax(scores, axis=-1).astype(v.dtype)
        return jnp.einsum('bhqk,bkhd->bqhd', p, v)

    out = lax.map(attend_block, (to_blocks(q_nope), to_blocks(q_rope), jnp.arange(nb)))
    return jnp.moveaxis(out, 0, 1).reshape(b, s, MLA_HEADS * MLA_V)


def causal_depthwise_conv(x, w, bias):
    y = lax.conv_general_dilated(
        x, w[:, None, :].astype(x.dtype), window_strides=(1,), padding=[(CONV_K - 1, 0)],
        dimension_numbers=('NWC', 'WIO', 'NWC'), feature_group_count=x.shape[-1])
    return y + bias.astype(x.dtype)


def mlstm_chunkwise(q, k, v, i_pre, f_pre):
    b, s, nh, dk = q.shape
    dv = v.shape[-1]
    n = s // CHUNK
    f32 = jnp.float32

    def chunks(t):
        return t.astype(f32).reshape(b, n, CHUNK, nh, -1).transpose(0, 3, 1, 2, 4)

    def gate_chunks(t):
        return t.astype(f32).reshape(b, n, CHUNK, nh).transpose(0, 3, 1, 2)

    q, k, v = chunks(q), chunks(k), chunks(v)
    log_i = gate_chunks(i_pre)
    log_f = jax.nn.log_sigmoid(gate_chunks(f_pre))
    cum_f = jnp.cumsum(log_f, axis=-1)
    causal = jnp.tril(jnp.ones((CHUNK, CHUNK), dtype=bool))
    log_d = jnp.where(causal, cum_f[..., :, None] - cum_f[..., None, :] + log_i[..., None, :], -jnp.inf)
    m_intra = jnp.max(log_d, axis=-1)
    f_total = cum_f[..., -1]
    log_w = f_total[..., None] - cum_f + log_i
    m_local = jnp.max(log_w, axis=-1)
    w_local = jnp.exp(log_w - m_local[..., None])
    c_local = jnp.einsum('bhnlk,bhnlv->bhnkv', k * w_local[..., None], v)
    n_local = jnp.einsum('bhnl,bhnlk->bhnk', w_local, k)

    def step(carry, inp):
        c_prev, n_prev, m_prev = carry
        c_loc, n_loc, m_loc, f_tot = inp
        m_new = jnp.maximum(f_tot + m_prev, m_loc)
        decay = jnp.exp(f_tot + m_prev - m_new)
        scale = jnp.exp(m_loc - m_new)
        c_new = decay[..., None, None] * c_prev + scale[..., None, None] * c_loc
        n_new = decay[..., None] * n_prev + scale[..., None] * n_loc
        return (c_new, n_new, m_new), (c_prev, n_prev, m_prev)

    init = (jnp.zeros((b, nh, dk, dv), f32), jnp.zeros((b, nh, dk), f32), jnp.zeros((b, nh), f32))
    xs = (jnp.moveaxis(c_local, 2, 0), jnp.moveaxis(n_local, 2, 0),
          jnp.moveaxis(m_local, 2, 0), jnp.moveaxis(f_total, 2, 0))
    _, (c_in, n_in, m_in) = lax.scan(step, init, xs)
    c_in = jnp.moveaxis(c_in, 0, 2)
    n_in = jnp.moveaxis(n_in, 0, 2)
    m_in = jnp.moveaxis(m_in, 0, 2)
    log_inter = cum_f + m_in[..., None]
    m_t = jnp.maximum(log_inter, m_intra)
    inter = jnp.exp(log_inter - m_t)
    p = jnp.einsum('bhnlk,bhnsk->bhnls', q, k) * jnp.exp(log_d - m_t[..., None])
    num = inter[..., None] * jnp.einsum('bhnlk,bhnkv->bhnlv', q, c_in) + jnp.einsum('bhnls,bhnsv->bhnlv', p, v)
    den = inter * jnp.einsum('bhnlk,bhnk->bhnl', q, n_in) + jnp.sum(p, axis=-1)
    h = num / jnp.maximum(jnp.abs(den), jnp.exp(-m_t))[..., None]
    return h.transpose(0, 2, 3, 1, 4).reshape(b, s, nh, dv)


def mlstm_mixer(cols, conv_w, conv_b, gate_b, h_norm_g):
    b, s, _ = cols.shape
    qk_w = 2 * ML_HEADS * ML_DK
    hv = ML_HEADS * ML_DV
    qk = jax.nn.silu(causal_depthwise_conv(cols[..., :qk_w], conv_w, conv_b))
    q, k = jnp.split(qk, 2, axis=-1)
    v = cols[..., qk_w:qk_w + hv]
    o = cols[..., qk_w + hv:qk_w + 2 * hv]
    gates = cols[..., qk_w + 2 * hv:].astype(jnp.float32) + gate_b.astype(jnp.float32)
    i_pre, f_pre = gates[..., :ML_HEADS], gates[..., ML_HEADS:]
    h = mlstm_chunkwise(q.reshape(b, s, ML_HEADS, ML_DK),
                        k.reshape(b, s, ML_HEADS, ML_DK) * ML_DK ** -0.5,
                        v.reshape(b, s, ML_HEADS, ML_DV), i_pre, f_pre)
    h = rms_norm(h, h_norm_g.reshape(ML_HEADS, ML_DV)).astype(cols.dtype)
    return h.reshape(b, s, hv) * jax.nn.sigmoid(o)


def moe_ffn(x, w_router, b_router, w_gate, w_up, w_down):
    logits = (x @ w_router).astype(jnp.float32) + b_router.astype(jnp.float32)
    top_vals, top_idx = lax.top_k(logits, TOP_K)
    weights = jax.nn.softmax(top_vals, axis=-1)
    gates = jnp.sum(jax.nn.one_hot(top_idx, N_EXPERTS, dtype=jnp.float32) * weights[..., None], axis=-2)
    out = jnp.zeros_like(x)
    for e in range(N_EXPERTS):
        out = out + gates[..., e:e + 1].astype(x.dtype) * swiglu(x, w_gate[e], w_up[e], w_down[e])
    return out


def _n_layers_of(start, period):
    return len(range(start, DEPTH, period))


def setup_inputs(seed: int = 0) -> dict:
    key = jax.random.key(seed)
    keys = list(jax.random.split(key, 40))
    f32 = jnp.float32

    def nk():
        return keys.pop()

    def w(shape, fan_in):
        return jax.random.normal(nk(), shape, f32) * fan_in ** -0.5

    def gain(shape):
        return 1.0 + 0.02 * jax.random.normal(nk(), shape, f32)

    na, nb, nc = (_n_layers_of(m, N_MIXERS) for m in range(N_MIXERS))
    nd, ne = _n_layers_of(0, 2), _n_layers_of(1, 2)
    gm_in = 2 * GM_WIDTH + MEM_WIDTH
    mla_in = MLA_Q_LORA + MLA_KV_LORA + MLA_ROPE + MEM_WIDTH
    ml_qk = 2 * ML_HEADS * ML_DK
    ml_in = ml_qk + 2 * ML_HEADS * ML_DV + 2 * ML_HEADS + MEM_WIDTH

    x = jax.random.normal(nk(), (BATCH, SEQ, D_MODEL), f32)
    mem = jax.random.normal(nk(), (BATCH, MEM_TOKENS, D_MODEL), f32)
    positions = (jnp.arange(SEQ, dtype=jnp.int32)[None, :]
                 + jax.random.randint(nk(), (BATCH, 1), 0, 1024, dtype=jnp.int32))
    i_bias = -1.0 + 0.1 * jax.random.normal(nk(), (nc, ML_HEADS), f32)
    f_bias = 3.0 + 3.0 * jax.random.uniform(nk(), (nc, ML_HEADS), f32)
    return {
        'x': x,
        'mem': mem,
        'positions': positions,
        'attn_norm': gain((DEPTH, D_MODEL)),
        'mem_norm': gain((DEPTH, D_MODEL)),
        'w_mem_kv': w((DEPTH, D_MODEL, 2 * MEM_WIDTH), D_MODEL),
        'ffn_norm': gain((DEPTH, D_MODEL)),
        'final_norm': gain((D_MODEL,)),
        'gm_w_in': w((na, D_MODEL, gm_in), D_MODEL),
        'gm_v_norm': gain((na, GM_WIDTH)),
        'gm_w_s': w((na, GM_GROUPS, CHUNK, CHUNK), CHUNK),
        'gm_b_s': 1.0 + 0.1 * jax.random.normal(nk(), (na, GM_GROUPS, CHUNK), f32),
        'gm_w_out': w((na, GM_WIDTH + MEM_WIDTH, D_MODEL), GM_WIDTH + MEM_WIDTH),
        'mla_w_in': w((nb, D_MODEL, mla_in), D_MODEL),
        'mla_q_norm': gain((nb, MLA_Q_LORA)),
        'mla_w_uq': w((nb, MLA_Q_LORA, MLA_HEADS * (MLA_NOPE + MLA_ROPE)), MLA_Q_LORA),
        'mla_kv_norm': gain((nb, MLA_KV_LORA)),
        'mla_w_ukv': w((nb, MLA_KV_LORA, MLA_HEADS * (MLA_NOPE + MLA_V)), MLA_KV_LORA),
        'mla_w_out': w((nb, MLA_HEADS * MLA_V + MEM_WIDTH, D_MODEL), MLA_HEADS * MLA_V + MEM_WIDTH),
        'ml_w_in': w((nc, D_MODEL, ml_in), D_MODEL),
        'ml_conv_w': w((nc, CONV_K, ml_qk), CONV_K),
        'ml_conv_b': 0.02 * jax.random.normal(nk(), (nc, ml_qk), f32),
        'ml_gate_b': jnp.concatenate([i_bias, f_bias], axis=-1),
        'ml_h_norm': gain((nc, ML_HEADS * ML_DV)),
        'ml_w_out': w((nc, ML_HEADS * ML_DV + MEM_WIDTH, D_MODEL), ML_HEADS * ML_DV + MEM_WIDTH),
        'ff_w_gate': w((nd, D_MODEL, FF_DENSE), D_MODEL),
        'ff_w_up': w((nd, D_MODEL, FF_DENSE), D_MODEL),
        'ff_w_down': w((nd, FF_DENSE, D_MODEL), FF_DENSE),
        'moe_w_router': w((ne, D_MODEL, N_EXPERTS), D_MODEL),
        'moe_b_router': 0.01 * jax.random.normal(nk(), (ne, N_EXPERTS), f32),
        'moe_w_gate': w((ne, N_EXPERTS, D_MODEL, FF_EXPERT), D_MODEL),
        'moe_w_up': w((ne, N_EXPERTS, D_MODEL, FF_EXPERT), D_MODEL),
        'moe_w_down': w((ne, N_EXPERTS, FF_EXPERT, D_MODEL), FF_EXPERT),
    }


def reference(x, mem, positions, attn_norm, mem_norm, w_mem_kv, ffn_norm, final_norm,
              gm_w_in, gm_v_norm, gm_w_s, gm_b_s, gm_w_out,
              mla_w_in, mla_q_norm, mla_w_uq, mla_kv_norm, mla_w_ukv, mla_w_out,
              ml_w_in, ml_conv_w, ml_conv_b, ml_gate_b, ml_h_norm, ml_w_out,
              ff_w_gate, ff_w_up, ff_w_down,
              moe_w_router, moe_b_router, moe_w_gate, moe_w_up, moe_w_down):
    h = x
    for layer in range(DEPTH):
        kind, j = layer % N_MIXERS, layer // N_MIXERS
        xn = rms_norm(h, attn_norm[layer])
        if kind == 0:
            proj = xn @ gm_w_in[j]
            mix = gmlp_mixer(proj[..., :-MEM_WIDTH], gm_v_norm[j], gm_w_s[j], gm_b_s[j])
            w_out = gm_w_out[j]
        elif kind == 1:
            proj = xn @ mla_w_in[j]
            mix = mla_mixer(proj[..., :-MEM_WIDTH], positions, mla_q_norm[j], mla_w_uq[j],
                            mla_kv_norm[j], mla_w_ukv[j])
            w_out = mla_w_out[j]
        else:
            proj = xn @ ml_w_in[j]
            mix = mlstm_mixer(proj[..., :-MEM_WIDTH], ml_conv_w[j], ml_conv_b[j], ml_gate_b[j], ml_h_norm[j])
            w_out = ml_w_out[j]
        mem_out = memory_cross_attention(proj[..., -MEM_WIDTH:], mem, mem_norm[layer], w_mem_kv[layer])
        h = h + jnp.concatenate([mix, mem_out], axis=-1) @ w_out
        xn = rms_norm(h, ffn_norm[layer])
        c = layer // 2
        if layer % 2 == 0:
            h = h + swiglu(xn, ff_w_gate[c], ff_w_up[c], ff_w_down[c])
        else:
            h = h + moe_ffn(xn, moe_w_router[c], moe_b_router[c], moe_w_gate[c], moe_w_up[c], moe_w_down[c])
    return rms_norm(h, final_norm)
```

```python
import functools

import jax
import jax.numpy as jnp
from jax import lax
from jax.experimental import pallas as pl
from jax.experimental.pallas import tpu as pltpu

F32 = jnp.float32
BF16 = jnp.bfloat16

NORM_EPS = 1e-6
CHUNK = 128
LANES = 128
MEM_HEADS = 4
MEM_HEAD_DIM = 128
MEM_WIDTH = MEM_HEADS * MEM_HEAD_DIM
GM_GROUPS = 8
GM_GROUP_DIM = 128
GM_WIDTH = GM_GROUPS * GM_GROUP_DIM
MLA_HEADS = 8
MLA_NOPE = 128
MLA_ROPE = 64
MLA_V = 128
MLA_Q_LORA = 768
MLA_KV_LORA = 256
MLA_QK_PAD = 256
ROPE_THETA = 10000.0
ML_HEADS = 4
ML_DV = 256
ML_DK = 128
ML_AUG = ML_DV + LANES
CONV_K = 4
N_EXPERTS = 8
TOP_K = 2

VMEM_LIMIT = 56 * 1024 * 1024

NT_DIMS = (((1,), (1,)), ((), ()))


def _params(semantics, vmem=VMEM_LIMIT):
    return pltpu.CompilerParams(dimension_semantics=semantics, vmem_limit_bytes=vmem)


def _resident(shape):
    nd = len(shape)
    return pl.BlockSpec(shape, lambda *_: (0,) * nd, pipeline_mode=pl.Buffered(1))


def _dot(a, b):
    return jnp.dot(a, b, preferred_element_type=F32)


def _dot_nt(a, b):
    return lax.dot_general(a, b, NT_DIMS, preferred_element_type=F32)


def _rms(x, g):
    return x * lax.rsqrt(jnp.mean(x * x, axis=-1, keepdims=True) + NORM_EPS) * g


def _silu(x):
    return x * jax.nn.sigmoid(x)


def _gelu(x):
    return 0.5 * x * (1.0 + lax.erf(x * 0.5 ** 0.5))


def _one_hot(mask):
    return jnp.where(mask, 1.0, 0.0).astype(BF16)


def _memkv_kernel(mem_ref, g_ref, w_ref, kv_ref):
    xn = _rms(mem_ref[0], g_ref[0]).astype(BF16)
    kv_ref[0, 0] = _dot(xn, w_ref[0]).astype(BF16)


def _memkv(mem, mem_norm, w_mem_kv):
    b, m, d = mem.shape
    depth = mem_norm.shape[0]
    n = w_mem_kv.shape[-1]
    return pl.pallas_call(
        _memkv_kernel,
        grid=(depth, b),
        in_specs=[
            pl.BlockSpec((1, m, d), lambda l, i: (i, 0, 0)),
            pl.BlockSpec((1, 1, d), lambda l, i: (l, 0, 0)),
            pl.BlockSpec((1, d, n), lambda l, i: (l, 0, 0)),
        ],
        out_specs=pl.BlockSpec((1, 1, m, n), lambda l, i: (l, i, 0, 0)),
        out_shape=jax.ShapeDtypeStruct((depth, b, m, n), BF16),
        compiler_params=_params(("arbitrary", "arbitrary")),
        name="memkv",
    )(mem, mem_norm.reshape(depth, 1, d), w_mem_kv.astype(BF16))


def _mem_attention(q, kv_ref):
    outs = []
    for hd in range(MEM_HEADS):
        lo = hd * MEM_HEAD_DIM
        qh = q[:, lo:lo + MEM_HEAD_DIM].astype(BF16)
        kh = kv_ref[0, 0, :, lo:lo + MEM_HEAD_DIM]
        vh = kv_ref[0, 0, :, MEM_WIDTH + lo:MEM_WIDTH + lo + MEM_HEAD_DIM]
        s = _dot_nt(qh, kh) * MEM_HEAD_DIM ** -0.5
        p = jnp.exp(s - jnp.max(s, axis=-1, keepdims=True))
        outs.append(_dot(p.astype(BF16), vh) / jnp.sum(p, axis=-1, keepdims=True))
    return jnp.concatenate(outs, axis=-1)


def _gmlp_layer_kernel(h_ref, g_ref, win_ref, vg_ref, ws_ref, bs_ref, kv_ref, wout_ref,
                       o_ref, proj_ref, mix_ref):
    tm = h_ref.shape[0]
    x = h_ref[...]
    xn = _rms(x, g_ref[...]).astype(BF16)
    proj_ref[...] = _dot(xn, win_ref[...])
    row = lax.broadcasted_iota(jnp.int32, (CHUNK, CHUNK), 0)
    col = lax.broadcasted_iota(jnp.int32, (CHUNK, CHUNK), 1)
    causal = col <= row
    for g in range(GM_GROUPS):
        lo = g * GM_GROUP_DIM
        u = _gelu(proj_ref[:, lo:lo + GM_GROUP_DIM])
        v = _gelu(proj_ref[:, GM_WIDTH + lo:GM_WIDTH + lo + GM_GROUP_DIM])
        vn = _rms(v, vg_ref[:, lo:lo + GM_GROUP_DIM]).astype(BF16)
        w = jnp.where(causal, ws_ref[g], 0.0).astype(BF16)
        for c in range(tm // CHUNK):
            r = c * CHUNK
            mixed = _dot(w, vn[r:r + CHUNK]) + bs_ref[g]
            mix_ref[r:r + CHUNK, lo:lo + GM_GROUP_DIM] = (u[r:r + CHUNK] * mixed).astype(BF16)
    mem_o = _mem_attention(proj_ref[:, 2 * GM_WIDTH:2 * GM_WIDTH + MEM_WIDTH], kv_ref)
    mix_ref[:, GM_WIDTH:GM_WIDTH + MEM_WIDTH] = mem_o.astype(BF16)
    o_ref[...] = x + _dot(mix_ref[...], wout_ref[...])


def _gmlp_layer(h, seq, layer, norm_g, w_in, v_norm, w_s, b_s, kv_all, w_out, tm=256):
    t, d = h.shape
    n_in = w_in.shape[1]
    n_mix = w_out.shape[0]
    tiles_per_seq = seq // tm
    m, kvw = kv_all.shape[2], kv_all.shape[3]
    bs_full = jnp.broadcast_to(b_s[:, :, None], (GM_GROUPS, CHUNK, GM_GROUP_DIM)).astype(F32)
    return pl.pallas_call(
        _gmlp_layer_kernel,
        grid=(t // tm,),
        in_specs=[
            pl.BlockSpec((tm, d), lambda i: (i, 0)),
            _resident((1, d)),
            _resident((d, n_in)),
            _resident((1, GM_WIDTH)),
            _resident((GM_GROUPS, CHUNK, CHUNK)),
            _resident((GM_GROUPS, CHUNK, GM_GROUP_DIM)),
            pl.BlockSpec((1, 1, m, kvw), lambda i: (layer, i // tiles_per_seq, 0, 0)),
            _resident((n_mix, d)),
        ],
        out_specs=pl.BlockSpec((tm, d), lambda i: (i, 0)),
        out_shape=jax.ShapeDtypeStruct((t, d), F32),
        scratch_shapes=[pltpu.VMEM((tm, n_in), F32), pltpu.VMEM((tm, n_mix), BF16)],
        compiler_params=_params(("arbitrary",)),
        name="gmlp_layer",
    )(h, norm_g.reshape(1, d), w_in.astype(BF16), v_norm.reshape(1, GM_WIDTH), w_s,
      bs_full, kv_all, w_out.astype(BF16))


def _ffn_kernel(h_ref, g_ref, wg_ref, wu_ref, wd_ref, o_ref, hid_ref, *, fc):
    x = h_ref[...]
    xn = _rms(x, g_ref[...]).astype(BF16)
    f = wg_ref.shape[1]
    for c in range(f // fc):
        lo = c * fc
        gate = _dot(xn, wg_ref[:, lo:lo + fc])
        up = _dot(xn, wu_ref[:, lo:lo + fc])
        hid_ref[:, lo:lo + fc] = (_silu(gate) * up).astype(BF16)
    o_ref[...] = x + _dot(hid_ref[...], wd_ref[...])


def _ffn_layer(h, norm_g, w_gate, w_up, w_down, tm=512, fc=256):
    t, d = h.shape
    f = w_gate.shape[1]
    return pl.pallas_call(
        functools.partial(_ffn_kernel, fc=fc),
        grid=(t // tm,),
        in_specs=[
            pl.BlockSpec((tm, d), lambda i: (i, 0)),
            _resident((1, d)),
            _resident((d, f)),
            _resident((d, f)),
            _resident((f, d)),
        ],
        out_specs=pl.BlockSpec((tm, d), lambda i: (i, 0)),
        out_shape=jax.ShapeDtypeStruct((t, d), F32),
        scratch_shapes=[pltpu.VMEM((tm, f), BF16)],
        compiler_params=_params(("arbitrary",)),
        name="ffn_layer",
    )(h, norm_g.reshape(1, d), w_gate.astype(BF16), w_up.astype(BF16), w_down.astype(BF16))


def _router_kernel(h_ref, g_ref, wr_ref, br_ref, gates_ref):
    xn = _rms(h_ref[...], g_ref[...])
    logits = lax.dot_general(wr_ref[...], xn, NT_DIMS, preferred_element_type=F32,
                             precision=lax.Precision.HIGHEST) + br_ref[...]
    idx = lax.broadcasted_iota(jnp.int32, logits.shape, 0)
    m1 = jnp.max(logits, axis=0, keepdims=True)
    i1 = jnp.min(jnp.where(logits == m1, idx, N_EXPERTS), axis=0, keepdims=True)
    first = idx == i1
    rest = jnp.where(first, -jnp.inf, logits)
    m2 = jnp.max(rest, axis=0, keepdims=True)
    i2 = jnp.min(jnp.where(rest == m2, idx, N_EXPERTS), axis=0, keepdims=True)
    second = idx == i2
    e2 = jnp.exp(m2 - m1)
    denom = 1.0 + e2
    gates_ref[...] = jnp.where(first, 1.0 / denom, 0.0) + jnp.where(second, e2 / denom, 0.0)


def _router(h, norm_g, w_router, b_router, tm=512):
    t, d = h.shape
    return pl.pallas_call(
        _router_kernel,
        grid=(t // tm,),
        in_specs=[
            pl.BlockSpec((tm, d), lambda i: (i, 0)),
            _resident((1, d)),
            _resident((N_EXPERTS, d)),
            _resident((N_EXPERTS, 1)),
        ],
        out_specs=pl.BlockSpec((N_EXPERTS, tm), lambda i: (0, i)),
        out_shape=jax.ShapeDtypeStruct((N_EXPERTS, t), F32),
        compiler_params=_params(("arbitrary",)),
        name="moe_router",
    )(h, norm_g.reshape(1, d), w_router.T, b_router.reshape(N_EXPERTS, 1))


def _moe_kernel(cnt_ref, h_ref, g_ref, grow_ref, gcol_ref, wg_ref, wu_ref, wd_ref, fg_ref,
                o_ref, xn_ref, acc_ref, rrow_ref, rcol_ref, *, cs, final_norm):
    i = pl.program_id(0)
    e = pl.program_id(1)
    tm = h_ref.shape[0]

    @pl.when(e == 0)
    def _():
        xn_ref[...] = _rms(h_ref[...], g_ref[...]).astype(BF16)
        acc_ref[...] = jnp.zeros_like(acc_ref)
        r = lax.broadcasted_iota(jnp.int32, (tm, tm), 0)
        c = lax.broadcasted_iota(jnp.int32, (tm, tm), 1)
        rrow_ref[...] = _dot(_one_hot(grow_ref[...] != 0.0), _one_hot(r < c))
        rcol_ref[...] = _dot(_one_hot(c < r), _one_hot(gcol_ref[...] != 0.0))

    gate_row = grow_ref[pl.ds(e, 1), :]
    rank_row = rrow_ref[pl.ds(e, 1), :]
    lane = lax.broadcasted_iota(jnp.int32, (tm, LANES), 1)
    gate_col = jnp.sum(jnp.where(lane == e, gcol_ref[...], 0.0), axis=-1, keepdims=True)
    rank_col = jnp.sum(jnp.where(lane == e, rcol_ref[...], 0.0), axis=-1, keepdims=True)
    slot_s = lax.broadcasted_iota(jnp.int32, (cs, tm), 0).astype(F32)
    slot_l = lax.broadcasted_iota(jnp.int32, (tm, cs), 1).astype(F32)
    n_blocks = (cnt_ref[i, e] + cs - 1) // cs

    def block(b, carry):
        off = jnp.full((1, 1), b * cs, jnp.int32).astype(F32)
        gather = _one_hot(jnp.logical_and(rank_row - off == slot_s, gate_row != 0.0))
        xg = _dot(gather, xn_ref[...]).astype(BF16)
        hid = (_silu(_dot(xg, wg_ref[0])) * _dot(xg, wu_ref[0])).astype(BF16)
        y = _dot(hid, wd_ref[0]).astype(BF16)
        scatter = _one_hot(jnp.logical_and(rank_col - off == slot_l, gate_col != 0.0))
        acc_ref[...] += gate_col * _dot(scatter, y)
        return carry

    lax.fori_loop(0, n_blocks, block, 0)

    @pl.when(e == pl.num_programs(1) - 1)
    def _():
        out = h_ref[...] + acc_ref[...]
        if final_norm:
            out = _rms(out, fg_ref[...])
        o_ref[...] = out


def _moe_layer(h, norm_g, w_router, b_router, w_gate, w_up, w_down, final_g, tm=1024, cs=256):
    t, d = h.shape
    f = w_gate.shape[-1]
    gates_row = _router(h, norm_g, w_router, b_router)
    gates_col = jnp.pad(gates_row.T, ((0, 0), (0, LANES - N_EXPERTS)))
    counts = jnp.sum((gates_row != 0.0).reshape(N_EXPERTS, t // tm, tm), axis=-1,
                     dtype=jnp.int32).T
    final_norm = final_g is not None
    fg = (final_g if final_norm else jnp.ones((d,), F32)).reshape(1, d)
    grid_spec = pltpu.PrefetchScalarGridSpec(
        num_scalar_prefetch=1,
        grid=(t // tm, N_EXPERTS),
        in_specs=[
            pl.BlockSpec((tm, d), lambda i, e, cnt: (i, 0)),
            pl.BlockSpec((1, d), lambda i, e, cnt: (0, 0)),
            pl.BlockSpec((N_EXPERTS, tm), lambda i, e, cnt: (0, i)),
            pl.BlockSpec((tm, LANES), lambda i, e, cnt: (i, 0)),
            pl.BlockSpec((1, d, f), lambda i, e, cnt: (e, 0, 0)),
            pl.BlockSpec((1, d, f), lambda i, e, cnt: (e, 0, 0)),
            pl.BlockSpec((1, f, d), lambda i, e, cnt: (e, 0, 0)),
            pl.BlockSpec((1, d), lambda i, e, cnt: (0, 0)),
        ],
        out_specs=pl.BlockSpec((tm, d), lambda i, e, cnt: (i, 0)),
        scratch_shapes=[
            pltpu.VMEM((tm, d), BF16),
            pltpu.VMEM((tm, d), F32),
            pltpu.VMEM((N_EXPERTS, tm), F32),
            pltpu.VMEM((tm, LANES), F32),
        ],
    )
    return pl.pallas_call(
        functools.partial(_moe_kernel, cs=cs, final_norm=final_norm),
        grid_spec=grid_spec,
        out_shape=jax.ShapeDtypeStruct((t, d), F32),
        compiler_params=_params(("arbitrary", "arbitrary")),
        name="moe_layer",
    )(counts, h, norm_g.reshape(1, d), gates_row, gates_col, w_gate.astype(BF16),
      w_up.astype(BF16), w_down.astype(BF16), fg)


def _mla_proj_kernel(h_ref, g_ref, pos_ref, freq_ref, win_ref, qg_ref, wuq_ref, kvg_ref, wukv_ref,
                     kv_ref, q_out, k_out, v_out, memo_out):
    xn = _rms(h_ref[...], g_ref[...]).astype(BF16)
    proj = _dot(xn, win_ref[...])
    c_q = proj[:, :MLA_Q_LORA]
    c_kv = proj[:, MLA_Q_LORA:MLA_Q_LORA + MLA_KV_LORA]
    lo = MLA_Q_LORA + MLA_KV_LORA
    mem_q = proj[:, lo:lo + MEM_WIDTH]
    k_rope = proj[:, lo + MEM_WIDTH:lo + MEM_WIDTH + LANES]
    memo_out[...] = _mem_attention(mem_q, kv_ref).astype(BF16)

    ang = pos_ref[...].astype(F32) * freq_ref[...]
    lane = lax.broadcasted_iota(jnp.int32, ang.shape, 1)
    cos = jnp.where(lane < MLA_ROPE, jnp.cos(ang), 0.0)
    sin = jnp.sin(ang)
    sin = jnp.where(lane < MLA_ROPE // 2, -sin, jnp.where(lane < MLA_ROPE, sin, 0.0))

    def rope(r):
        return r * cos + pltpu.roll(r, MLA_ROPE // 2, 1) * sin

    scale = (MLA_NOPE + MLA_ROPE) ** -0.5
    q = _dot(_rms(c_q, qg_ref[...]).astype(BF16), wuq_ref[...])
    kvu = _dot(_rms(c_kv, kvg_ref[...]).astype(BF16), wukv_ref[...])
    k_r = rope(k_rope).astype(BF16)
    for hd in range(MLA_HEADS):
        qlo = hd * MLA_QK_PAD
        q_out[:, qlo:qlo + MLA_NOPE] = (q[:, qlo:qlo + MLA_NOPE] * scale).astype(BF16)
        q_out[:, qlo + MLA_NOPE:qlo + MLA_QK_PAD] = (
            rope(q[:, qlo + MLA_NOPE:qlo + MLA_QK_PAD]) * scale).astype(BF16)
        k_out[:, qlo:qlo + MLA_NOPE] = kvu[:, hd * MLA_NOPE:(hd + 1) * MLA_NOPE].astype(BF16)
        k_out[:, qlo + MLA_NOPE:qlo + MLA_QK_PAD] = k_r
    v_out[...] = kvu[:, MLA_HEADS * MLA_NOPE:].astype(BF16)


def _flash_kernel(q_ref, k_ref, v_ref, o_ref, m_ref, l_ref, acc_ref):
    i = pl.program_id(1)
    j = pl.program_id(2)
    tq, tk = q_ref.shape[0], k_ref.shape[0]

    @pl.when(j == 0)
    def _():
        m_ref[...] = jnp.full_like(m_ref, -jnp.inf)
        l_ref[...] = jnp.zeros_like(l_ref)
        acc_ref[...] = jnp.zeros_like(acc_ref)

    def step(masked):
        if masked:
            row = lax.broadcasted_iota(jnp.int32, (tq, tk), 0)
            col = lax.broadcasted_iota(jnp.int32, (tq, tk), 1)
            keep = col <= row
        for hd in range(MLA_HEADS):
            q = q_ref[:, hd * MLA_QK_PAD:(hd + 1) * MLA_QK_PAD]
            k = k_ref[:, hd * MLA_QK_PAD:(hd + 1) * MLA_QK_PAD]
            v = v_ref[:, hd * MLA_V:(hd + 1) * MLA_V]
            s = _dot_nt(q, k)
            if masked:
                s = jnp.where(keep, s, -jnp.inf)
            m_prev = m_ref[hd]
            m_new = jnp.maximum(m_prev, jnp.max(s, axis=-1, keepdims=True))
            alpha = jnp.exp(m_prev - m_new)
            p = jnp.exp(s - m_new)
            l_ref[hd] = alpha * l_ref[hd] + jnp.sum(p, axis=-1, keepdims=True)
            acc_ref[hd] = alpha * acc_ref[hd] + _dot(p.astype(BF16), v)
            m_ref[hd] = m_new

    @pl.when(j < i)
    def _():
        step(False)

    @pl.when(j == i)
    def _():
        step(True)

    @pl.when(j == pl.num_programs(2) - 1)
    def _():
        for hd in range(MLA_HEADS):
            o_ref[:, hd * MLA_V:(hd + 1) * MLA_V] = (acc_ref[hd] / l_ref[hd]).astype(BF16)


def _outproj_kernel(h_ref, a_ref, m_ref, wa_ref, wm_ref, o_ref):
    o_ref[...] = h_ref[...] + _dot(a_ref[...], wa_ref[...]) + _dot(m_ref[...], wm_ref[...])


def _mla_layer(h, batch, seq, layer, positions, norm_g, w_in, q_norm, w_uq, kv_norm, w_ukv,
               kv_all, w_out, tm=256, tq=512):
    t, d = h.shape
    half = MLA_ROPE // 2
    lo = MLA_Q_LORA + MLA_KV_LORA
    w_kr = w_in[:, lo:lo + MLA_ROPE]
    w_in_r = jnp.concatenate([w_in[:, :lo], w_in[:, lo + MLA_ROPE:], w_kr, w_kr], axis=1).astype(BF16)
    wq = w_uq.reshape(MLA_Q_LORA, MLA_HEADS, MLA_NOPE + MLA_ROPE)
    wq_r = jnp.concatenate([wq, wq[:, :, MLA_NOPE:]], axis=2).reshape(
        MLA_Q_LORA, MLA_HEADS * MLA_QK_PAD).astype(BF16)
    wkv = w_ukv.reshape(MLA_KV_LORA, MLA_HEADS, MLA_NOPE + MLA_V)
    wkv_r = jnp.concatenate([wkv[:, :, :MLA_NOPE].reshape(MLA_KV_LORA, -1),
                             wkv[:, :, MLA_NOPE:].reshape(MLA_KV_LORA, -1)], axis=1).astype(BF16)
    inv_freq = 1.0 / (ROPE_THETA ** (jnp.arange(0, MLA_ROPE, 2, dtype=F32) / MLA_ROPE))
    freq = jnp.tile(inv_freq, LANES // half).reshape(1, LANES)
    n_in = w_in_r.shape[1]
    tiles_per_seq = seq // tm
    m, kvw = kv_all.shape[2], kv_all.shape[3]
    qk_w = MLA_HEADS * MLA_QK_PAD
    v_w = MLA_HEADS * MLA_V
    q, k, v, memo = pl.pallas_call(
        _mla_proj_kernel,
        grid=(t // tm,),
        in_specs=[
            pl.BlockSpec((tm, d), lambda i: (i, 0)),
            _resident((1, d)),
            pl.BlockSpec((tm, 1), lambda i: (i, 0)),
            _resident((1, LANES)),
            _resident((d, n_in)),
            _resident((1, MLA_Q_LORA)),
            _resident((MLA_Q_LORA, qk_w)),
            _resident((1, MLA_KV_LORA)),
            _resident((MLA_KV_LORA, 2 * v_w)),
            pl.BlockSpec((1, 1, m, kvw), lambda i: (layer, i // tiles_per_seq, 0, 0)),
        ],
        out_specs=[
            pl.BlockSpec((tm, qk_w), lambda i: (i, 0)),
            pl.BlockSpec((tm, qk_w), lambda i: (i, 0)),
            pl.BlockSpec((tm, v_w), lambda i: (i, 0)),
            pl.BlockSpec((tm, MEM_WIDTH), lambda i: (i, 0)),
        ],
        out_shape=[
            jax.ShapeDtypeStruct((t, qk_w), BF16),
            jax.ShapeDtypeStruct((t, qk_w), BF16),
            jax.ShapeDtypeStruct((t, v_w), BF16),
            jax.ShapeDtypeStruct((t, MEM_WIDTH), BF16),
        ],
        compiler_params=_params(("arbitrary",)),
        name="mla_proj",
    )(h, norm_g.reshape(1, d), positions.reshape(t, 1), freq, w_in_r, q_norm.reshape(1, -1), wq_r,
      kv_norm.reshape(1, -1), wkv_r, kv_all)

    nq = seq // tq
    attn = pl.pallas_call(
        _flash_kernel,
        grid=(batch, nq, nq),
        in_specs=[
            pl.BlockSpec((tq, qk_w), lambda b, i, j: (b * nq + i, 0)),
            pl.BlockSpec((tq, qk_w), lambda b, i, j: (b * nq + jnp.minimum(i, j), 0)),
            pl.BlockSpec((tq, v_w), lambda b, i, j: (b * nq + jnp.minimum(i, j), 0)),
        ],
        out_specs=pl.BlockSpec((tq, v_w), lambda b, i, j: (b * nq + i, 0)),
        out_shape=jax.ShapeDtypeStruct((t, v_w), BF16),
        scratch_shapes=[
            pltpu.VMEM((MLA_HEADS, tq, 1), F32),
            pltpu.VMEM((MLA_HEADS, tq, 1), F32),
            pltpu.VMEM((MLA_HEADS, tq, MLA_V), F32),
        ],
        compiler_params=_params(("arbitrary", "arbitrary", "arbitrary")),
        name="mla_flash",
    )(q, k, v)

    to = 512
    w_out_b = w_out.astype(BF16)
    return pl.pallas_call(
        _outproj_kernel,
        grid=(t // to,),
        in_specs=[
            pl.BlockSpec((to, d), lambda i: (i, 0)),
            pl.BlockSpec((to, v_w), lambda i: (i, 0)),
            pl.BlockSpec((to, MEM_WIDTH), lambda i: (i, 0)),
            _resident((v_w, d)),
            _resident((MEM_WIDTH, d)),
        ],
        out_specs=pl.BlockSpec((to, d), lambda i: (i, 0)),
        out_shape=jax.ShapeDtypeStruct((t, d), F32),
        compiler_params=_params(("arbitrary",)),
        name="mla_outproj",
    )(h, attn, memo, w_out_b[:v_w], w_out_b[v_w:])


def _mlstm_layer_kernel(h_ref, g_ref, win_ref, cw_ref, cb_ref, gb_ref, hg_ref, kv_ref, wout_ref,
                        o_ref, qk_ref, c_ref, m_ref, mix_ref):
    tm = h_ref.shape[0]
    qk_w = 2 * ML_HEADS * ML_DK
    hv = ML_HEADS * ML_DV
    pad = 8

    @pl.when(pl.program_id(1) == 0)
    def _():
        qk_ref[0:pad, :] = jnp.zeros((pad, qk_w), F32)
        c_ref[...] = jnp.zeros_like(c_ref)
        m_ref[...] = jnp.zeros_like(m_ref)

    x = h_ref[...]
    xn = _rms(x, g_ref[...]).astype(BF16)
    proj = _dot(xn, win_ref[...])
    qk_ref[pad:pad + tm, :] = proj[:, :qk_w]
    conv = cb_ref[...]
    for j in range(CONV_K):
        conv = conv + cw_ref[j:j + 1, :] * qk_ref[pad - (CONV_K - 1) + j:pad - (CONV_K - 1) + j + tm, :]
    qk_ref[0:pad, :] = qk_ref[tm:tm + pad, :]
    qk = _silu(conv)
    v_all = proj[:, qk_w:qk_w + hv]
    o_all = proj[:, qk_w + hv:qk_w + 2 * hv]
    mem_q = proj[:, qk_w + 2 * hv:qk_w + 2 * hv + MEM_WIDTH]
    gates = proj[:, qk_w + 2 * hv + MEM_WIDTH:] + gb_ref[...]

    lane = lax.broadcasted_iota(jnp.int32, (CHUNK, LANES), 1)
    row = lax.broadcasted_iota(jnp.int32, (CHUNK, CHUNK), 0)
    col = lax.broadcasted_iota(jnp.int32, (CHUNK, CHUNK), 1)
    causal = col <= row
    tril = _one_hot(causal)
    ones_blk = _one_hot(lane == 0)

    for c in range(tm // CHUNK):
        r0 = c * CHUNK
        gc = gates[r0:r0 + CHUNK]
        logs = jnp.where(lane >= ML_HEADS, jax.nn.log_sigmoid(gc), gc)
        hi = logs.astype(BF16)
        r1 = logs - hi.astype(F32)
        mid = r1.astype(BF16)
        low = (r1 - mid.astype(F32)).astype(BF16)
        cum = _dot(tril, hi) + _dot(tril, mid) + _dot(tril, low)
        col_vals = jnp.where(lane >= ML_HEADS, cum, logs)
        row_vals = col_vals.T
        for hd in range(ML_HEADS):
            q = qk[r0:r0 + CHUNK, hd * ML_DK:(hd + 1) * ML_DK].astype(BF16)
            k32 = qk[r0:r0 + CHUNK, qk_w // 2 + hd * ML_DK:qk_w // 2 + (hd + 1) * ML_DK] * ML_DK ** -0.5
            v = v_all[r0:r0 + CHUNK, hd * ML_DV:(hd + 1) * ML_DV].astype(BF16)
            v_aug = jnp.concatenate([v, ones_blk], axis=1)
            a_col = col_vals[:, ML_HEADS + hd:ML_HEADS + hd + 1]
            li_col = col_vals[:, hd:hd + 1]
            b_row = row_vals[ML_HEADS + hd:ML_HEADS + hd + 1, :]
            li_row = row_vals[hd:hd + 1, :]
            m_in = m_ref[hd]
            c_in = c_ref[hd]

            log_d = jnp.where(causal, a_col - b_row + li_row, -jnp.inf)
            m_intra = jnp.max(log_d, axis=-1, keepdims=True)
            log_inter = a_col + m_in
            m_t = jnp.maximum(log_inter, m_intra)
            inter = jnp.exp(log_inter - m_t)
            p = (_dot_nt(q, k32.astype(BF16)) * jnp.exp(log_d - m_t)).astype(BF16)
            numden = inter * _dot(q, c_in.astype(BF16)) + _dot(p, v_aug)
            den = numden[:, ML_DV:ML_DV + 1]
            h_out = numden[:, :ML_DV] / jnp.maximum(jnp.abs(den), jnp.exp(-m_t))
            h_n = _rms(h_out, hg_ref[:, hd * ML_DV:(hd + 1) * ML_DV])
            gate_o = jax.nn.sigmoid(o_all[r0:r0 + CHUNK, hd * ML_DV:(hd + 1) * ML_DV])
            mix_ref[r0:r0 + CHUNK, hd * ML_DV:(hd + 1) * ML_DV] = (h_n * gate_o).astype(BF16)

            f_tot = b_row[:, CHUNK - 1:CHUNK]
            log_w = f_tot - a_col + li_col
            m_loc = jnp.max(log_w, axis=0, keepdims=True)
            kw_t = (k32 * jnp.exp(log_w - m_loc)).T.astype(BF16)
            c_loc = _dot(kw_t, v_aug)
            m_new = jnp.maximum(f_tot + m_in, m_loc)
            c_ref[hd] = jnp.exp(f_tot + m_in - m_new) * c_in + jnp.exp(m_loc - m_new) * c_loc
            m_ref[hd] = m_new

    mix_ref[:, hv:hv + MEM_WIDTH] = _mem_attention(mem_q, kv_ref).astype(BF16)
    o_ref[...] = x + _dot(mix_ref[...], wout_ref[...])


def _mlstm_layer(h, batch, seq, layer, norm_g, w_in, conv_w, conv_b, gate_b, h_norm, kv_all, w_out,
                 tm=256):
    t, d = h.shape
    qk_w = 2 * ML_HEADS * ML_DK
    hv = ML_HEADS * ML_DV
    n_gate = 2 * ML_HEADS
    lo = qk_w + 2 * hv
    w_in_r = jnp.concatenate(
        [w_in[:, :lo], w_in[:, lo + n_gate:], w_in[:, lo:lo + n_gate],
         jnp.zeros((d, LANES - n_gate), w_in.dtype)], axis=1).astype(BF16)
    gb = jnp.pad(gate_b, (0, LANES - n_gate)).reshape(1, LANES)
    n_in = w_in_r.shape[1]
    n_mix = w_out.shape[0]
    tiles = seq // tm
    m, kvw = kv_all.shape[2], kv_all.shape[3]
    return pl.pallas_call(
        _mlstm_layer_kernel,
        grid=(batch, tiles),
        in_specs=[
            pl.BlockSpec((tm, d), lambda b, i: (b * tiles + i, 0)),
            _resident((1, d)),
            _resident((d, n_in)),
            _resident((CONV_K, qk_w)),
            _resident((1, qk_w)),
            _resident((1, LANES)),
            _resident((1, hv)),
            pl.BlockSpec((1, 1, m, kvw), lambda b, i: (layer, b, 0, 0)),
            _resident((n_mix, d)),
        ],
        out_specs=pl.BlockSpec((tm, d), lambda b, i: (b * tiles + i, 0)),
        out_shape=jax.ShapeDtypeStruct((t, d), F32),
        scratch_shapes=[
            pltpu.VMEM((tm + 8, qk_w), F32),
            pltpu.VMEM((ML_HEADS, ML_DK, ML_AUG), F32),
            pltpu.VMEM((ML_HEADS, 1, 1), F32),
            pltpu.VMEM((tm, n_mix), BF16),
        ],
        compiler_params=_params(("arbitrary", "arbitrary")),
        name="mlstm_layer",
    )(h, norm_g.reshape(1, d), w_in_r, conv_w, conv_b.reshape(1, qk_w), gb, h_norm.reshape(1, hv),
      kv_all, w_out.astype(BF16))


def kernel(x, mem, positions, attn_norm, mem_norm, w_mem_kv, ffn_norm, final_norm, gm_w_in, gm_v_norm, gm_w_s, gm_b_s, gm_w_out, mla_w_in, mla_q_norm, mla_w_uq, mla_kv_norm, mla_w_ukv, mla_w_out, ml_w_in, ml_conv_w, ml_conv_b, ml_gate_b, ml_h_norm, ml_w_out, ff_w_gate, ff_w_up, ff_w_down, moe_w_router, moe_b_router, moe_w_gate, moe_w_up, moe_w_down):
    batch, seq, d = x.shape
    depth = attn_norm.shape[0]
    kv_all = _memkv(mem, mem_norm, w_mem_kv)
    h = x.reshape(batch * seq, d)
    for layer in range(depth):
        kind, j = layer % 3, layer // 3
        if kind == 0:
            h = _gmlp_layer(h, seq, layer, attn_norm[layer], gm_w_in[j], gm_v_norm[j], gm_w_s[j],
                            gm_b_s[j], kv_all, gm_w_out[j])
        elif kind == 1:
            h = _mla_layer(h, batch, seq, layer, positions, attn_norm[layer], mla_w_in[j],
                           mla_q_norm[j], mla_w_uq[j], mla_kv_norm[j], mla_w_ukv[j], kv_all,
                           mla_w_out[j])
        else:
            h = _mlstm_layer(h, batch, seq, layer, attn_norm[layer], ml_w_in[j], ml_conv_w[j],
                             ml_conv_b[j], ml_gate_b[j], ml_h_norm[j], kv_all, ml_w_out[j])
        c = layer // 2
        last = layer == depth - 1
        if layer % 2 == 0:
            h = _ffn_layer(h, ffn_norm[layer], ff_w_gate[c], ff_w_up[c], ff_w_down[c])
            if last:
                h = _final_norm(h, final_norm)
        else:
            h = _moe_layer(h, ffn_norm[layer], moe_w_router[c], moe_b_router[c], moe_w_gate[c],
                           moe_w_up[c], moe_w_down[c], final_norm if last else None)
    return h.reshape(batch, seq, d)


def _final_norm_kernel(h_ref, g_ref, o_ref):
    o_ref[...] = _rms(h_ref[...], g_ref[...])


def _final_norm(h, g, tm=1024):
    t, d = h.shape
    return pl.pallas_call(
        _final_norm_kernel,
        grid=(t // tm,),
        in_specs=[pl.BlockSpec((tm, d), lambda i: (i, 0)), _resident((1, d))],
        out_specs=pl.BlockSpec((tm, d), lambda i: (i, 0)),
        out_shape=jax.ShapeDtypeStruct((t, d), F32),
        compiler_params=_params(("arbitrary",)),
        name="final_norm",
    )(h, g.reshape(1, d))
```

```python
import functools

import jax
import jax.numpy as jnp
from jax import lax
from jax.experimental import pallas as pl
from jax.experimental.pallas import tpu as pltpu

F32 = jnp.float32
BF16 = jnp.bfloat16

NORM_EPS = 1e-6
CHUNK = 128
LANES = 128
MEM_HEADS = 4
MEM_HEAD_DIM = 128
MEM_WIDTH = MEM_HEADS * MEM_HEAD_DIM
GM_GROUPS = 8
GM_GROUP_DIM = 128
GM_WIDTH = GM_GROUPS * GM_GROUP_DIM
MLA_HEADS = 8
MLA_NOPE = 128
MLA_ROPE = 64
MLA_V = 128
MLA_Q_LORA = 768
MLA_KV_LORA = 256
MLA_QK_PAD = 256
ROPE_THETA = 10000.0
ML_HEADS = 4
ML_DV = 256
ML_DK = 128
ML_AUG = ML_DV + LANES
CONV_K = 4
N_EXPERTS = 8
TOP_K = 2

VMEM_LIMIT = 56 * 1024 * 1024

NT_DIMS = (((1,), (1,)), ((), ()))


def _params(semantics, vmem=VMEM_LIMIT):
    return pltpu.CompilerParams(dimension_semantics=semantics, vmem_limit_bytes=vmem)


def _resident(shape):
    nd = len(shape)
    return pl.BlockSpec(shape, lambda *_: (0,) * nd, pipeline_mode=pl.Buffered(1))


def _dot(a, b):
    return jnp.dot(a, b, preferred_element_type=F32)


def _dot_nt(a, b):
    return lax.dot_general(a, b, NT_DIMS, preferred_element_type=F32)


def _rms(x, g):
    return x * lax.rsqrt(jnp.mean(x * x, axis=-1, keepdims=True) + NORM_EPS) * g


def _silu(x):
    return x * jax.nn.sigmoid(x)


def _gelu(x):
    return 0.5 * x * (1.0 + lax.erf(x * 0.5 ** 0.5))


def _one_hot(mask):
    return jnp.where(mask, 1.0, 0.0).astype(BF16)


def _memkv_kernel(mem_ref, g_ref, w_ref, kv_ref):
    xn = _rms(mem_ref[0], g_ref[0]).astype(BF16)
    kv_ref[0, 0] = _dot(xn, w_ref[0]).astype(BF16)


def _memkv(mem, mem_norm, w_mem_kv):
    b, m, d = mem.shape
    depth = mem_norm.shape[0]
    n = w_mem_kv.shape[-1]
    return pl.pallas_call(
        _memkv_kernel,
        grid=(depth, b),
        in_specs=[
            pl.BlockSpec((1, m, d), lambda l, i: (i, 0, 0)),
            pl.BlockSpec((1, 1, d), lambda l, i: (l, 0, 0)),
            pl.BlockSpec((1, d, n), lambda l, i: (l, 0, 0)),
        ],
        out_specs=pl.BlockSpec((1, 1, m, n), lambda l, i: (l, i, 0, 0)),
        out_shape=jax.ShapeDtypeStruct((depth, b, m, n), BF16),
        compiler_params=_params(("arbitrary", "arbitrary")),
        name="memkv",
    )(mem, mem_norm.reshape(depth, 1, d), w_mem_kv.astype(BF16))


def _mem_attention(q, kv_ref):
    outs = []
    for hd in range(MEM_HEADS):
        lo = hd * MEM_HEAD_DIM
        qh = q[:, lo:lo + MEM_HEAD_DIM].astype(BF16)
        kh = kv_ref[0, 0, :, lo:lo + MEM_HEAD_DIM]
        vh = kv_ref[0, 0, :, MEM_WIDTH + lo:MEM_WIDTH + lo + MEM_HEAD_DIM]
        s = _dot_nt(qh, kh) * MEM_HEAD_DIM ** -0.5
        p = jnp.exp(s - jnp.max(s, axis=-1, keepdims=True))
        outs.append(_dot(p.astype(BF16), vh) / jnp.sum(p, axis=-1, keepdims=True))
    return jnp.concatenate(outs, axis=-1)


def _gmlp_layer_kernel(h_ref, g_ref, win_ref, vg_ref, ws_ref, bs_ref, kv_ref, wout_ref,
                       o_ref, proj_ref, mix_ref):
    tm = h_ref.shape[0]
    x = h_ref[...]
    xn = _rms(x, g_ref[...]).astype(BF16)
    proj_ref[...] = _dot(xn, win_ref[...])
    row = lax.broadcasted_iota(jnp.int32, (CHUNK, CHUNK), 0)
    col = lax.broadcasted_iota(jnp.int32, (CHUNK, CHUNK), 1)
    causal = col <= row
    for g in range(GM_GROUPS):
        lo = g * GM_GROUP_DIM
        u = _gelu(proj_ref[:, lo:lo + GM_GROUP_DIM])
        v = _gelu(proj_ref[:, GM_WIDTH + lo:GM_WIDTH + lo + GM_GROUP_DIM])
        vn = _rms(v, vg_ref[:, lo:lo + GM_GROUP_DIM]).astype(BF16)
        w = jnp.where(causal, ws_ref[g], 0.0).astype(BF16)
        for c in range(tm // CHUNK):
            r = c * CHUNK
            mixed = _dot(w, vn[r:r + CHUNK]) + bs_ref[g]
            mix_ref[r:r + CHUNK, lo:lo + GM_GROUP_DIM] = (u[r:r + CHUNK] * mixed).astype(BF16)
    mem_o = _mem_attention(proj_ref[:, 2 * GM_WIDTH:2 * GM_WIDTH + MEM_WIDTH], kv_ref)
    mix_ref[:, GM_WIDTH:GM_WIDTH + MEM_WIDTH] = mem_o.astype(BF16)
    o_ref[...] = x + _dot(mix_ref[...], wout_ref[...])


def _gmlp_layer(h, seq, layer, norm_g, w_in, v_norm, w_s, b_s, kv_all, w_out, tm=256):
    t, d = h.shape
    n_in = w_in.shape[1]
    n_mix = w_out.shape[0]
    tiles_per_seq = seq // tm
    m, kvw = kv_all.shape[2], kv_all.shape[3]
    bs_full = jnp.broadcast_to(b_s[:, :, None], (GM_GROUPS, CHUNK, GM_GROUP_DIM)).astype(F32)
    return pl.pallas_call(
        _gmlp_layer_kernel,
        grid=(t // tm,),
        in_specs=[
            pl.BlockSpec((tm, d), lambda i: (i, 0)),
            _resident((1, d)),
            _resident((d, n_in)),
            _resident((1, GM_WIDTH)),
            _resident((GM_GROUPS, CHUNK, CHUNK)),
            _resident((GM_GROUPS, CHUNK, GM_GROUP_DIM)),
            pl.BlockSpec((1, 1, m, kvw), lambda i: (layer, i // tiles_per_seq, 0, 0)),
            _resident((n_mix, d)),
        ],
        out_specs=pl.BlockSpec((tm, d), lambda i: (i, 0)),
        out_shape=jax.ShapeDtypeStruct((t, d), F32),
        scratch_shapes=[pltpu.VMEM((tm, n_in), F32), pltpu.VMEM((tm, n_mix), BF16)],
        compiler_params=_params(("arbitrary",)),
        name="gmlp_layer",
    )(h, norm_g.reshape(1, d), w_in.astype(BF16), v_norm.reshape(1, GM_WIDTH), w_s,
      bs_full, kv_all, w_out.astype(BF16))


def _ffn_kernel(h_ref, g_ref, wg_ref, wu_ref, wd_ref, o_ref, hid_ref, *, fc):
    x = h_ref[...]
    xn = _rms(x, g_ref[...]).astype(BF16)
    f = wg_ref.shape[1]
    for c in range(f // fc):
        lo = c * fc
        gate = _dot(xn, wg_ref[:, lo:lo + fc])
        up = _dot(xn, wu_ref[:, lo:lo + fc])
        hid_ref[:, lo:lo + fc] = (_silu(gate) * up).astype(BF16)
    o_ref[...] = x + _dot(hid_ref[...], wd_ref[...])


def _ffn_layer(h, norm_g, w_gate, w_up, w_down, tm=512, fc=256):
    t, d = h.shape
    f = w_gate.shape[1]
    return pl.pallas_call(
        functools.partial(_ffn_kernel, fc=fc),
        grid=(t // tm,),
        in_specs=[
            pl.BlockSpec((tm, d), lambda i: (i, 0)),
            _resident((1, d)),
            _resident((d, f)),
            _resident((d, f)),
            _resident((f, d)),
        ],
        out_specs=pl.BlockSpec((tm, d), lambda i: (i, 0)),
        out_shape=jax.ShapeDtypeStruct((t, d), F32),
        scratch_shapes=[pltpu.VMEM((tm, f), BF16)],
        compiler_params=_params(("arbitrary",)),
        name="ffn_layer",
    )(h, norm_g.reshape(1, d), w_gate.astype(BF16), w_up.astype(BF16), w_down.astype(BF16))


def _router_kernel(h_ref, g_ref, wr_ref, br_ref, gates_ref):
    xn = _rms(h_ref[...], g_ref[...])
    logits = lax.dot_general(wr_ref[...], xn, NT_DIMS, preferred_element_type=F32,
                             precision=lax.Precision.HIGHEST) + br_ref[...]
    idx = lax.broadcasted_iota(jnp.int32, logits.shape, 0)
    m1 = jnp.max(logits, axis=0, keepdims=True)
    i1 = jnp.min(jnp.where(logits == m1, idx, N_EXPERTS), axis=0, keepdims=True)
    first = idx == i1
    rest = jnp.where(first, -jnp.inf, logits)
    m2 = jnp.max(rest, axis=0, keepdims=True)
    i2 = jnp.min(jnp.where(rest == m2, idx, N_EXPERTS), axis=0, keepdims=True)
    second = idx == i2
    e2 = jnp.exp(m2 - m1)
    denom = 1.0 + e2
    gates_ref[...] = jnp.where(first, 1.0 / denom, 0.0) + jnp.where(second, e2 / denom, 0.0)


def _router(h, norm_g, w_router, b_router, tm=512):
    t, d = h.shape
    return pl.pallas_call(
        _router_kernel,
        grid=(t // tm,),
        in_specs=[
            pl.BlockSpec((tm, d), lambda i: (i, 0)),
            _resident((1, d)),
            _resident((N_EXPERTS, d)),
            _resident((N_EXPERTS, 1)),
        ],
        out_specs=pl.BlockSpec((N_EXPERTS, tm), lambda i: (0, i)),
        out_shape=jax.ShapeDtypeStruct((N_EXPERTS, t), F32),
        compiler_params=_params(("arbitrary",)),
        name="moe_router",
    )(h, norm_g.reshape(1, d), w_router.T, b_router.reshape(N_EXPERTS, 1))


def _moe_kernel(cnt_ref, h_ref, g_ref, grow_ref, gcol_ref, wg_ref, wu_ref, wd_ref, fg_ref,
                o_ref, xn_ref, acc_ref, rrow_ref, rcol_ref, *, cs, final_norm):
    i = pl.program_id(0)
    e = pl.program_id(1)
    tm = h_ref.shape[0]

    @pl.when(e == 0)
    def _():
        xn_ref[...] = _rms(h_ref[...], g_ref[...]).astype(BF16)
        acc_ref[...] = jnp.zeros_like(acc_ref)
        r = lax.broadcasted_iota(jnp.int32, (tm, tm), 0)
        c = lax.broadcasted_iota(jnp.int32, (tm, tm), 1)
        rrow_ref[...] = _dot(_one_hot(grow_ref[...] != 0.0), _one_hot(r < c))
        rcol_ref[...] = _dot(_one_hot(c < r), _one_hot(gcol_ref[...] != 0.0))

    gate_row = grow_ref[pl.ds(e, 1), :]
    rank_row = rrow_ref[pl.ds(e, 1), :]
    lane = lax.broadcasted_iota(jnp.int32, (tm, LANES), 1)
    gate_col = jnp.sum(jnp.where(lane == e, gcol_ref[...], 0.0), axis=-1, keepdims=True)
    rank_col = jnp.sum(jnp.where(lane == e, rcol_ref[...], 0.0), axis=-1, keepdims=True)
    slot_s = lax.broadcasted_iota(jnp.int32, (cs, tm), 0).astype(F32)
    slot_l = lax.broadcasted_iota(jnp.int32, (tm, cs), 1).astype(F32)
    n_blocks = (cnt_ref[i, e] + cs - 1) // cs

    def block(b, carry):
        off = jnp.full((1, 1), b * cs, jnp.int32).astype(F32)
        gather = _one_hot(jnp.logical_and(rank_row - off == slot_s, gate_row != 0.0))
        xg = _dot(gather, xn_ref[...]).astype(BF16)
        hid = (_silu(_dot(xg, wg_ref[0])) * _dot(xg, wu_ref[0])).astype(BF16)
        y = _dot(hid, wd_ref[0]).astype(BF16)
        scatter = _one_hot(jnp.logical_and(rank_col - off == slot_l, gate_col != 0.0))
        acc_ref[...] += gate_col * _dot(scatter, y)
        return carry

    lax.fori_loop(0, n_blocks, block, 0)

    @pl.when(e == pl.num_programs(1) - 1)
    def _():
        out = h_ref[...] + acc_ref[...]
        if final_norm:
            out = _rms(out, fg_ref[...])
        o_ref[...] = out


def _moe_layer(h, norm_g, w_router, b_router, w_gate, w_up, w_down, final_g, tm=1024, cs=256):
    t, d = h.shape
    f = w_gate.shape[-1]
    gates_row = _router(h, norm_g, w_router, b_router)
    gates_col = jnp.pad(gates_row.T, ((0, 0), (0, LANES - N_EXPERTS)))
    counts = jnp.sum((gates_row != 0.0).reshape(N_EXPERTS, t // tm, tm), axis=-1,
                     dtype=jnp.int32).T
    final_norm = final_g is not None
    fg = (final_g if final_norm else jnp.ones((d,), F32)).reshape(1, d)
    grid_spec = pltpu.PrefetchScalarGridSpec(
        num_scalar_prefetch=1,
        grid=(t // tm, N_EXPERTS),
        in_specs=[
            pl.BlockSpec((tm, d), lambda i, e, cnt: (i, 0)),
            pl.BlockSpec((1, d), lambda i, e, cnt: (0, 0)),
            pl.BlockSpec((N_EXPERTS, tm), lambda i, e, cnt: (0, i)),
            pl.BlockSpec((tm, LANES), lambda i, e, cnt: (i, 0)),
            pl.BlockSpec((1, d, f), lambda i, e, cnt: (e, 0, 0)),
            pl.BlockSpec((1, d, f), lambda i, e, cnt: (e, 0, 0)),
            pl.BlockSpec((1, f, d), lambda i, e, cnt: (e, 0, 0)),
            pl.BlockSpec((1, d), lambda i, e, cnt: (0, 0)),
        ],
        out_specs=pl.BlockSpec((tm, d), lambda i, e, cnt: (i, 0)),
        scratch_shapes=[
            pltpu.VMEM((tm, d), BF16),
            pltpu.VMEM((tm, d), F32),
            pltpu.VMEM((N_EXPERTS, tm), F32),
            pltpu.VMEM((tm, LANES), F32),
        ],
    )
    return pl.pallas_call(
        functools.partial(_moe_kernel, cs=cs, final_norm=final_norm),
        grid_spec=grid_spec,
        out_shape=jax.ShapeDtypeStruct((t, d), F32),
        compiler_params=_params(("arbitrary", "arbitrary")),
        name="moe_layer",
    )(counts, h, norm_g.reshape(1, d), gates_row, gates_col, w_gate.astype(BF16),
      w_up.astype(BF16), w_down.astype(BF16), fg)


def _mla_proj_kernel(h_ref, g_ref, pos_ref, freq_ref, win_ref, qg_ref, wuq_ref, kvg_ref, wukv_ref,
                     kv_ref, q_out, k_out, v_out, memo_out):
    xn = _rms(h_ref[...], g_ref[...]).astype(BF16)
    proj = _dot(xn, win_ref[...])
    c_q = proj[:, :MLA_Q_LORA]
    c_kv = proj[:, MLA_Q_LORA:MLA_Q_LORA + MLA_KV_LORA]
    lo = MLA_Q_LORA + MLA_KV_LORA
    mem_q = proj[:, lo:lo + MEM_WIDTH]
    k_rope = proj[:, lo + MEM_WIDTH:lo + MEM_WIDTH + LANES]
    memo_out[...] = _mem_attention(mem_q, kv_ref).astype(BF16)

    ang = pos_ref[...].astype(F32) * freq_ref[...]
    lane = lax.broadcasted_iota(jnp.int32, ang.shape, 1)
    cos = jnp.where(lane < MLA_ROPE, jnp.cos(ang), 0.0)
    sin = jnp.sin(ang)
    sin = jnp.where(lane < MLA_ROPE // 2, -sin, jnp.where(lane < MLA_ROPE, sin, 0.0))

    def rope(r):
        return r * cos + pltpu.roll(r, MLA_ROPE // 2, 1) * sin

    scale = (MLA_NOPE + MLA_ROPE) ** -0.5
    q = _dot(_rms(c_q, qg_ref[...]).astype(BF16), wuq_ref[...])
    kvu = _dot(_rms(c_kv, kvg_ref[...]).astype(BF16), wukv_ref[...])
    k_r = rope(k_rope).astype(BF16)
    for hd in range(MLA_HEADS):
        qlo = hd * MLA_QK_PAD
        q_out[:, qlo:qlo + MLA_NOPE] = (q[:, qlo:qlo + MLA_NOPE] * scale).astype(BF16)
        q_out[:, qlo + MLA_NOPE:qlo + MLA_QK_PAD] = (
            rope(q[:, qlo + MLA_NOPE:qlo + MLA_QK_PAD]) * scale).astype(BF16)
        k_out[:, qlo:qlo + MLA_NOPE] = kvu[:, hd * MLA_NOPE:(hd + 1) * MLA_NOPE].astype(BF16)
        k_out[:, qlo + MLA_NOPE:qlo + MLA_QK_PAD] = k_r
    v_out[...] = kvu[:, MLA_HEADS * MLA_NOPE:].astype(BF16)


def _flash_kernel(qi_ref, kj_ref, q_ref, k_ref, v_ref, o_ref, m_ref, acc_ref):
    step_id = pl.program_id(1)
    i = qi_ref[step_id]
    j = kj_ref[step_id]
    tq, tk = q_ref.shape[0], k_ref.shape[0]

    @pl.when(j == 0)
    def _():
        m_ref[...] = jnp.full_like(m_ref, -jnp.inf)
        acc_ref[...] = jnp.zeros_like(acc_ref)

    ones = jnp.ones((tk, MLA_V), BF16)

    def step(masked):
        if masked:
            row = lax.broadcasted_iota(jnp.int32, (tq, tk), 0)
            col = lax.broadcasted_iota(jnp.int32, (tq, tk), 1)
            keep = col <= row
        for hd in range(MLA_HEADS):
            q = q_ref[:, hd * MLA_QK_PAD:(hd + 1) * MLA_QK_PAD]
            k = k_ref[:, hd * MLA_QK_PAD:(hd + 1) * MLA_QK_PAD]
            v_aug = jnp.concatenate([v_ref[:, hd * MLA_V:(hd + 1) * MLA_V], ones], axis=1)
            s = _dot_nt(q, k)
            if masked:
                s = jnp.where(keep, s, -jnp.inf)
            m_prev = m_ref[hd]
            m_new = jnp.maximum(m_prev, jnp.max(s, axis=-1, keepdims=True))
            alpha = jnp.exp(m_prev - m_new)
            p = jnp.exp((s - jnp.tile(m_new, (1, tk // LANES))).astype(BF16))
            acc_ref[hd] = jnp.tile(alpha, (1, 2)) * acc_ref[hd] + _dot(p, v_aug)
            m_ref[hd] = m_new

    @pl.when(j < i)
    def _():
        step(False)

    @pl.when(j == i)
    def _():
        step(True)
        for hd in range(MLA_HEADS):
            acc = acc_ref[hd]
            o_ref[:, hd * MLA_V:(hd + 1) * MLA_V] = (acc[:, :MLA_V] / acc[:, MLA_V:]).astype(BF16)


def _outproj_kernel(h_ref, a_ref, m_ref, wa_ref, wm_ref, o_ref):
    o_ref[...] = h_ref[...] + _dot(a_ref[...], wa_ref[...]) + _dot(m_ref[...], wm_ref[...])


def _mla_layer(h, batch, seq, layer, positions, norm_g, w_in, q_norm, w_uq, kv_norm, w_ukv,
               kv_all, w_out, tm=256, tq=512):
    t, d = h.shape
    half = MLA_ROPE // 2
    lo = MLA_Q_LORA + MLA_KV_LORA
    w_kr = w_in[:, lo:lo + MLA_ROPE]
    w_in_r = jnp.concatenate([w_in[:, :lo], w_in[:, lo + MLA_ROPE:], w_kr, w_kr], axis=1).astype(BF16)
    wq = w_uq.reshape(MLA_Q_LORA, MLA_HEADS, MLA_NOPE + MLA_ROPE)
    wq_r = jnp.concatenate([wq, wq[:, :, MLA_NOPE:]], axis=2).reshape(
        MLA_Q_LORA, MLA_HEADS * MLA_QK_PAD).astype(BF16)
    wkv = w_ukv.reshape(MLA_KV_LORA, MLA_HEADS, MLA_NOPE + MLA_V)
    wkv_r = jnp.concatenate([wkv[:, :, :MLA_NOPE].reshape(MLA_KV_LORA, -1),
                             wkv[:, :, MLA_NOPE:].reshape(MLA_KV_LORA, -1)], axis=1).astype(BF16)
    inv_freq = 1.0 / (ROPE_THETA ** (jnp.arange(0, MLA_ROPE, 2, dtype=F32) / MLA_ROPE))
    freq = jnp.tile(inv_freq, LANES // half).reshape(1, LANES)
    n_in = w_in_r.shape[1]
    tiles_per_seq = seq // tm
    m, kvw = kv_all.shape[2], kv_all.shape[3]
    qk_w = MLA_HEADS * MLA_QK_PAD
    v_w = MLA_HEADS * MLA_V
    q, k, v, memo = pl.pallas_call(
        _mla_proj_kernel,
        grid=(t // tm,),
        in_specs=[
            pl.BlockSpec((tm, d), lambda i: (i, 0)),
            _resident((1, d)),
            pl.BlockSpec((tm, 1), lambda i: (i, 0)),
            _resident((1, LANES)),
            _resident((d, n_in)),
            _resident((1, MLA_Q_LORA)),
            _resident((MLA_Q_LORA, qk_w)),
            _resident((1, MLA_KV_LORA)),
            _resident((MLA_KV_LORA, 2 * v_w)),
            pl.BlockSpec((1, 1, m, kvw), lambda i: (layer, i // tiles_per_seq, 0, 0)),
        ],
        out_specs=[
            pl.BlockSpec((tm, qk_w), lambda i: (i, 0)),
            pl.BlockSpec((tm, qk_w), lambda i: (i, 0)),
            pl.BlockSpec((tm, v_w), lambda i: (i, 0)),
            pl.BlockSpec((tm, MEM_WIDTH), lambda i: (i, 0)),
        ],
        out_shape=[
            jax.ShapeDtypeStruct((t, qk_w), BF16),
            jax.ShapeDtypeStruct((t, qk_w), BF16),
            jax.ShapeDtypeStruct((t, v_w), BF16),
            jax.ShapeDtypeStruct((t, MEM_WIDTH), BF16),
        ],
        compiler_params=_params(("arbitrary",)),
        name="mla_proj",
    )(h, norm_g.reshape(1, d), positions.reshape(t, 1), freq, w_in_r, q_norm.reshape(1, -1), wq_r,
      kv_norm.reshape(1, -1), wkv_r, kv_all)

    nq = seq // tq
    pairs = [(i, j) for i in range(nq) for j in range(i + 1)]
    qi = jnp.array([p[0] for p in pairs], jnp.int32)
    kj = jnp.array([p[1] for p in pairs], jnp.int32)
    attn = pl.pallas_call(
        _flash_kernel,
        grid_spec=pltpu.PrefetchScalarGridSpec(
            num_scalar_prefetch=2,
            grid=(batch, len(pairs)),
            in_specs=[
                pl.BlockSpec((tq, qk_w), lambda b, s, qi, kj: (b * nq + qi[s], 0)),
                pl.BlockSpec((tq, qk_w), lambda b, s, qi, kj: (b * nq + kj[s], 0)),
                pl.BlockSpec((tq, v_w), lambda b, s, qi, kj: (b * nq + kj[s], 0)),
            ],
            out_specs=pl.BlockSpec((tq, v_w), lambda b, s, qi, kj: (b * nq + qi[s], 0)),
            scratch_shapes=[
                pltpu.VMEM((MLA_HEADS, tq, LANES), F32),
                pltpu.VMEM((MLA_HEADS, tq, 2 * MLA_V), F32),
            ],
        ),
        out_shape=jax.ShapeDtypeStruct((t, v_w), BF16),
        compiler_params=_params(("arbitrary", "arbitrary")),
        name="mla_flash",
    )(qi, kj, q, k, v)

    to = 512
    w_out_b = w_out.astype(BF16)
    return pl.pallas_call(
        _outproj_kernel,
        grid=(t // to,),
        in_specs=[
            pl.BlockSpec((to, d), lambda i: (i, 0)),
            pl.BlockSpec((to, v_w), lambda i: (i, 0)),
            pl.BlockSpec((to, MEM_WIDTH), lambda i: (i, 0)),
            _resident((v_w, d)),
            _resident((MEM_WIDTH, d)),
        ],
        out_specs=pl.BlockSpec((to, d), lambda i: (i, 0)),
        out_shape=jax.ShapeDtypeStruct((t, d), F32),
        compiler_params=_params(("arbitrary",)),
        name="mla_outproj",
    )(h, attn, memo, w_out_b[:v_w], w_out_b[v_w:])


def _mlstm_layer_kernel(h_ref, g_ref, win_ref, cw_ref, cb_ref, gb_ref, hg_ref, kv_ref, wout_ref,
                        o_ref, qk_ref, c_ref, m_ref, mix_ref):
    tm = h_ref.shape[0]
    qk_w = 2 * ML_HEADS * ML_DK
    hv = ML_HEADS * ML_DV
    pad = 8

    @pl.when(pl.program_id(1) == 0)
    def _():
        qk_ref[0:pad, :] = jnp.zeros((pad, qk_w), F32)
        c_ref[...] = jnp.zeros_like(c_ref)
        m_ref[...] = jnp.zeros_like(m_ref)

    x = h_ref[...]
    xn = _rms(x, g_ref[...]).astype(BF16)
    proj = _dot(xn, win_ref[...])
    qk_ref[pad:pad + tm, :] = proj[:, :qk_w]
    conv = cb_ref[...]
    for j in range(CONV_K):
        conv = conv + cw_ref[j:j + 1, :] * qk_ref[pad - (CONV_K - 1) + j:pad - (CONV_K - 1) + j + tm, :]
    qk_ref[0:pad, :] = qk_ref[tm:tm + pad, :]
    qk = _silu(conv)
    v_all = proj[:, qk_w:qk_w + hv]
    o_all = proj[:, qk_w + hv:qk_w + 2 * hv]
    mem_q = proj[:, qk_w + 2 * hv:qk_w + 2 * hv + MEM_WIDTH]
    gates = proj[:, qk_w + 2 * hv + MEM_WIDTH:] + gb_ref[...]

    lane = lax.broadcasted_iota(jnp.int32, (CHUNK, LANES), 1)
    row = lax.broadcasted_iota(jnp.int32, (CHUNK, CHUNK), 0)
    col = lax.broadcasted_iota(jnp.int32, (CHUNK, CHUNK), 1)
    causal = col <= row
    tril = _one_hot(causal)
    ones_blk = _one_hot(lane == 0)

    for c in range(tm // CHUNK):
        r0 = c * CHUNK
        gc = gates[r0:r0 + CHUNK]
        logs = jnp.where(lane >= ML_HEADS, jax.nn.log_sigmoid(gc), gc)
        hi = logs.astype(BF16)
        r1 = logs - hi.astype(F32)
        mid = r1.astype(BF16)
        low = (r1 - mid.astype(F32)).astype(BF16)
        cum = _dot(tril, hi) + _dot(tril, mid) + _dot(tril, low)
        col_vals = jnp.where(lane >= ML_HEADS, cum, logs)
        row_vals = col_vals.T
        for hd in range(ML_HEADS):
            q = qk[r0:r0 + CHUNK, hd * ML_DK:(hd + 1) * ML_DK].astype(BF16)
            k32 = qk[r0:r0 + CHUNK, qk_w // 2 + hd * ML_DK:qk_w // 2 + (hd + 1) * ML_DK] * ML_DK ** -0.5
            v = v_all[r0:r0 + CHUNK, hd * ML_DV:(hd + 1) * ML_DV].astype(BF16)
            v_aug = jnp.concatenate([v, ones_blk], axis=1)
            a_col = col_vals[:, ML_HEADS + hd:ML_HEADS + hd + 1]
            li_col = col_vals[:, hd:hd + 1]
            b_row = row_vals[ML_HEADS + hd:ML_HEADS + hd + 1, :]
            li_row = row_vals[hd:hd + 1, :]
            m_in = m_ref[hd]
            c_in = c_ref[hd]

            log_d = jnp.where(causal, a_col - b_row + li_row, -jnp.inf)
            m_intra = jnp.max(log_d, axis=-1, keepdims=True)
            log_inter = a_col + m_in
            m_t = jnp.maximum(log_inter, m_intra)
            inter = jnp.exp(log_inter - m_t)
            p = (_dot_nt(q, k32.astype(BF16)) * jnp.exp(log_d - m_t)).astype(BF16)
            numden = inter * _dot(q, c_in.astype(BF16)) + _dot(p, v_aug)
            den = numden[:, ML_DV:ML_DV + 1]
            h_out = numden[:, :ML_DV] / jnp.maximum(jnp.abs(den), jnp.exp(-m_t))
            h_n = _rms(h_out, hg_ref[:, hd * ML_DV:(hd + 1) * ML_DV])
            gate_o = jax.nn.sigmoid(o_all[r0:r0 + CHUNK, hd * ML_DV:(hd + 1) * ML_DV])
            mix_ref[r0:r0 + CHUNK, hd * ML_DV:(hd + 1) * ML_DV] = (h_n * gate_o).astype(BF16)

            f_tot = b_row[:, CHUNK - 1:CHUNK]
            log_w = f_tot - a_col + li_col
            m_loc = jnp.max(log_w, axis=0, keepdims=True)
            kw_t = (k32 * jnp.exp(log_w - m_loc)).T.astype(BF16)
            c_loc = _dot(kw_t, v_aug)
            m_new = jnp.maximum(f_tot + m_in, m_loc)
            c_ref[hd] = jnp.exp(f_tot + m_in - m_new) * c_in + jnp.exp(m_loc - m_new) * c_loc
            m_ref[hd] = m_new

    mix_ref[:, hv:hv + MEM_WIDTH] = _mem_attention(mem_q, kv_ref).astype(BF16)
    o_ref[...] = x + _dot(mix_ref[...], wout_ref[...])


def _mlstm_layer(h, batch, seq, layer, norm_g, w_in, conv_w, conv_b, gate_b, h_norm, kv_all, w_out,
                 tm=256):
    t, d = h.shape
    qk_w = 2 * ML_HEADS * ML_DK
    hv = ML_HEADS * ML_DV
    n_gate = 2 * ML_HEADS
    lo = qk_w + 2 * hv
    w_in_r = jnp.concatenate(
        [w_in[:, :lo], w_in[:, lo + n_gate:], w_in[:, lo:lo + n_gate],
         jnp.zeros((d, LANES - n_gate), w_in.dtype)], axis=1).astype(BF16)
    gb = jnp.pad(gate_b, (0, LANES - n_gate)).reshape(1, LANES)
    n_in = w_in_r.shape[1]
    n_mix = w_out.shape[0]
    tiles = seq // tm
    m, kvw = kv_all.shape[2], kv_all.shape[3]
    return pl.pallas_call(
        _mlstm_layer_kernel,
        grid=(batch, tiles),
        in_specs=[
            pl.BlockSpec((tm, d), lambda b, i: (b * tiles + i, 0)),
            _resident((1, d)),
            _resident((d, n_in)),
            _resident((CONV_K, qk_w)),
            _resident((1, qk_w)),
            _resident((1, LANES)),
            _resident((1, hv)),
            pl.BlockSpec((1, 1, m, kvw), lambda b, i: (layer, b, 0, 0)),
            _resident((n_mix, d)),
        ],
        out_specs=pl.BlockSpec((tm, d), lambda b, i: (b * tiles + i, 0)),
        out_shape=jax.ShapeDtypeStruct((t, d), F32),
        scratch_shapes=[
            pltpu.VMEM((tm + 8, qk_w), F32),
            pltpu.VMEM((ML_HEADS, ML_DK, ML_AUG), F32),
            pltpu.VMEM((ML_HEADS, 1, 1), F32),
            pltpu.VMEM((tm, n_mix), BF16),
        ],
        compiler_params=_params(("arbitrary", "arbitrary")),
        name="mlstm_layer",
    )(h, norm_g.reshape(1, d), w_in_r, conv_w, conv_b.reshape(1, qk_w), gb, h_norm.reshape(1, hv),
      kv_all, w_out.astype(BF16))


def kernel(x, mem, positions, attn_norm, mem_norm, w_mem_kv, ffn_norm, final_norm, gm_w_in, gm_v_norm, gm_w_s, gm_b_s, gm_w_out, mla_w_in, mla_q_norm, mla_w_uq, mla_kv_norm, mla_w_ukv, mla_w_out, ml_w_in, ml_conv_w, ml_conv_b, ml_gate_b, ml_h_norm, ml_w_out, ff_w_gate, ff_w_up, ff_w_down, moe_w_router, moe_b_router, moe_w_gate, moe_w_up, moe_w_down):
    batch, seq, d = x.shape
    depth = attn_norm.shape[0]
    kv_all = _memkv(mem, mem_norm, w_mem_kv)
    h = x.reshape(batch * seq, d)
    for layer in range(depth):
        kind, j = layer % 3, layer // 3
        if kind == 0:
            h = _gmlp_layer(h, seq, layer, attn_norm[layer], gm_w_in[j], gm_v_norm[j], gm_w_s[j],
                            gm_b_s[j], kv_all, gm_w_out[j])
        elif kind == 1:
            h = _mla_layer(h, batch, seq, layer, positions, attn_norm[layer], mla_w_in[j],
                           mla_q_norm[j], mla_w_uq[j], mla_kv_norm[j], mla_w_ukv[j], kv_all,
                           mla_w_out[j])
        else:
            h = _mlstm_layer(h, batch, seq, layer, attn_norm[layer], ml_w_in[j], ml_conv_w[j],
                             ml_conv_b[j], ml_gate_b[j], ml_h_norm[j], kv_all, ml_w_out[j])
        c = layer // 2
        last = layer == depth - 1
        if layer % 2 == 0:
            h = _ffn_layer(h, ffn_norm[layer], ff_w_gate[c], ff_w_up[c], ff_w_down[c])
            if last:
                h = _final_norm(h, final_norm)
        else:
            h = _moe_layer(h, ffn_norm[layer], moe_w_router[c], moe_b_router[c], moe_w_gate[c],
                           moe_w_up[c], moe_w_down[c], final_norm if last else None)
    return h.reshape(batch, seq, d)


def _final_norm_kernel(h_ref, g_ref, o_ref):
    o_ref[...] = _rms(h_ref[...], g_ref[...])


def _final_norm(h, g, tm=1024):
    t, d = h.shape
    return pl.pallas_call(
        _final_norm_kernel,
        grid=(t // tm,),
        in_specs=[pl.BlockSpec((tm, d), lambda i: (i, 0)), _resident((1, d))],
        out_specs=pl.BlockSpec((tm, d), lambda i: (i, 0)),
        out_shape=jax.ShapeDtypeStruct((t, d), F32),
        compiler_params=_params(("arbitrary",)),
        name="final_norm",
    )(h, g.reshape(1, d))
```

```python
import functools

import jax
import jax.numpy as jnp
from jax import lax
from jax.experimental import pallas as pl
from jax.experimental.pallas import tpu as pltpu

F32 = jnp.float32
BF16 = jnp.bfloat16

NORM_EPS = 1e-6
CHUNK = 128
LANES = 128
MEM_HEADS = 4
MEM_HEAD_DIM = 128
MEM_WIDTH = MEM_HEADS * MEM_HEAD_DIM
GM_GROUPS = 8
GM_GROUP_DIM = 128
GM_WIDTH = GM_GROUPS * GM_GROUP_DIM
MLA_HEADS = 8
MLA_NOPE = 128
MLA_ROPE = 64
MLA_V = 128
MLA_Q_LORA = 768
MLA_KV_LORA = 256
MLA_QK_PAD = 256
ROPE_THETA = 10000.0
ML_HEADS = 4
ML_DV = 256
ML_DK = 128
ML_AUG = ML_DV + LANES
CONV_K = 4
N_EXPERTS = 8
TOP_K = 2

VMEM_LIMIT = 56 * 1024 * 1024

NT_DIMS = (((1,), (1,)), ((), ()))


def _params(semantics, vmem=VMEM_LIMIT):
    return pltpu.CompilerParams(dimension_semantics=semantics, vmem_limit_bytes=vmem)


def _resident(shape, index=None):
    index = (0,) * len(shape) if index is None else index
    return pl.BlockSpec(shape, lambda *_: index, pipeline_mode=pl.Buffered(1))


def _dot(a, b):
    return jnp.dot(a, b, preferred_element_type=F32)


def _dot_nt(a, b):
    return lax.dot_general(a, b, NT_DIMS, preferred_element_type=F32)


def _rms(x, g):
    return x * lax.rsqrt(jnp.mean(x * x, axis=-1, keepdims=True) + NORM_EPS) * g


def _silu(x):
    return x * jax.nn.sigmoid(x)


def _gelu(x):
    return 0.5 * x * (1.0 + lax.erf(x * 0.5 ** 0.5))


def _one_hot(mask):
    return jnp.where(mask, 1.0, 0.0).astype(BF16)


def _memkv_kernel(mem_ref, g_ref, w_ref, kv_ref):
    xn = _rms(mem_ref[0], g_ref[0]).astype(BF16)
    kv_ref[0, 0] = _dot(xn, w_ref[0]).astype(BF16)


def _memkv(mem, mem_norm, w_mem_kv):
    b, m, d = mem.shape
    depth = mem_norm.shape[0]
    n = w_mem_kv.shape[-1]
    return pl.pallas_call(
        _memkv_kernel,
        grid=(depth, b),
        in_specs=[
            pl.BlockSpec((1, m, d), lambda l, i: (i, 0, 0)),
            pl.BlockSpec((1, 1, d), lambda l, i: (l, 0, 0)),
            pl.BlockSpec((1, d, n), lambda l, i: (l, 0, 0)),
        ],
        out_specs=pl.BlockSpec((1, 1, m, n), lambda l, i: (l, i, 0, 0)),
        out_shape=jax.ShapeDtypeStruct((depth, b, m, n), BF16),
        compiler_params=_params(("arbitrary", "arbitrary")),
        name="memkv",
    )(mem, mem_norm.reshape(depth, 1, d), w_mem_kv.astype(BF16))


def _mem_attention(q, kv_ref):
    outs = []
    for hd in range(MEM_HEADS):
        lo = hd * MEM_HEAD_DIM
        qh = q[:, lo:lo + MEM_HEAD_DIM].astype(BF16)
        kh = kv_ref[0, 0, :, lo:lo + MEM_HEAD_DIM]
        vh = kv_ref[0, 0, :, MEM_WIDTH + lo:MEM_WIDTH + lo + MEM_HEAD_DIM]
        s = _dot_nt(qh, kh) * MEM_HEAD_DIM ** -0.5
        p = jnp.exp(s - jnp.max(s, axis=-1, keepdims=True))
        outs.append(_dot(p.astype(BF16), vh) / jnp.sum(p, axis=-1, keepdims=True))
    return jnp.concatenate(outs, axis=-1)


def _gmlp_layer_kernel(h_ref, g_ref, win_ref, vg_ref, ws_ref, bs_ref, kv_ref, wout_ref,
                       o_ref, proj_ref, mix_ref):
    tm = h_ref.shape[0]
    x = h_ref[...]
    xn = _rms(x, g_ref[...]).astype(BF16)
    proj_ref[...] = _dot(xn, win_ref[...])
    row = lax.broadcasted_iota(jnp.int32, (CHUNK, CHUNK), 0)
    col = lax.broadcasted_iota(jnp.int32, (CHUNK, CHUNK), 1)
    causal = col <= row
    for g in range(GM_GROUPS):
        lo = g * GM_GROUP_DIM
        u = _gelu(proj_ref[:, lo:lo + GM_GROUP_DIM])
        v = _gelu(proj_ref[:, GM_WIDTH + lo:GM_WIDTH + lo + GM_GROUP_DIM])
        vn = _rms(v, vg_ref[:, lo:lo + GM_GROUP_DIM]).astype(BF16)
        w = jnp.where(causal, ws_ref[g], 0.0).astype(BF16)
        for c in range(tm // CHUNK):
            r = c * CHUNK
            mixed = _dot(w, vn[r:r + CHUNK]) + bs_ref[g]
            mix_ref[r:r + CHUNK, lo:lo + GM_GROUP_DIM] = (u[r:r + CHUNK] * mixed).astype(BF16)
    mem_o = _mem_attention(proj_ref[:, 2 * GM_WIDTH:2 * GM_WIDTH + MEM_WIDTH], kv_ref)
    mix_ref[:, GM_WIDTH:GM_WIDTH + MEM_WIDTH] = mem_o.astype(BF16)
    o_ref[...] = x + _dot(mix_ref[...], wout_ref[...])


def _gmlp_layer(h, seq, layer, norm_g, w_in, v_norm, w_s, b_s, kv_all, w_out, tm=256):
    t, d = h.shape
    n_in = w_in.shape[1]
    n_mix = w_out.shape[0]
    tiles_per_seq = seq // tm
    m, kvw = kv_all.shape[2], kv_all.shape[3]
    bs_full = jnp.broadcast_to(b_s[:, :, None], (GM_GROUPS, CHUNK, GM_GROUP_DIM)).astype(F32)
    return pl.pallas_call(
        _gmlp_layer_kernel,
        grid=(t // tm,),
        in_specs=[
            pl.BlockSpec((tm, d), lambda i: (i, 0)),
            _resident((1, d)),
            _resident((d, n_in)),
            _resident((1, GM_WIDTH)),
            _resident((GM_GROUPS, CHUNK, CHUNK)),
            _resident((GM_GROUPS, CHUNK, GM_GROUP_DIM)),
            pl.BlockSpec((1, 1, m, kvw), lambda i: (layer, i // tiles_per_seq, 0, 0)),
            _resident((n_mix, d)),
        ],
        out_specs=pl.BlockSpec((tm, d), lambda i: (i, 0)),
        out_shape=jax.ShapeDtypeStruct((t, d), F32),
        scratch_shapes=[pltpu.VMEM((tm, n_in), F32), pltpu.VMEM((tm, n_mix), BF16)],
        compiler_params=_params(("arbitrary",)),
        name="gmlp_layer",
    )(h, norm_g.reshape(1, d), w_in.astype(BF16), v_norm.reshape(1, GM_WIDTH), w_s,
      bs_full, kv_all, w_out.astype(BF16))


def _ffn_kernel(h_ref, g_ref, wg_ref, wu_ref, wd_ref, o_ref, hid_ref, *, fc):
    x = h_ref[...]
    xn = _rms(x, g_ref[...]).astype(BF16)
    f = wg_ref.shape[-1]
    for c in range(f // fc):
        lo = c * fc
        gate = _dot(xn, wg_ref[0, :, lo:lo + fc])
        up = _dot(xn, wu_ref[0, :, lo:lo + fc])
        hid_ref[:, lo:lo + fc] = (_silu(gate) * up).astype(BF16)
    o_ref[...] = x + _dot(hid_ref[...], wd_ref[0])


def _ffn_layer(h, norm_g, w_gate, w_up, w_down, c, tm=512, fc=256):
    t, d = h.shape
    f = w_gate.shape[-1]
    return pl.pallas_call(
        functools.partial(_ffn_kernel, fc=fc),
        grid=(t // tm,),
        in_specs=[
            pl.BlockSpec((tm, d), lambda i: (i, 0)),
            _resident((1, d)),
            _resident((1, d, f), (c, 0, 0)),
            _resident((1, d, f), (c, 0, 0)),
            _resident((1, f, d), (c, 0, 0)),
        ],
        out_specs=pl.BlockSpec((tm, d), lambda i: (i, 0)),
        out_shape=jax.ShapeDtypeStruct((t, d), F32),
        scratch_shapes=[pltpu.VMEM((tm, f), BF16)],
        compiler_params=_params(("arbitrary",)),
        name="ffn_layer",
    )(h, norm_g.reshape(1, d), w_gate, w_up, w_down)


MOE_SLOT_BLOCK = 256
MOE_SLOT_VARIANTS = (256, 288, 320)


def _split_bf16(x):
    hi = x.astype(BF16)
    return hi, (x - hi.astype(F32)).astype(BF16)


def _top2_gates(logits, lane):
    m1 = jnp.max(logits, axis=-1, keepdims=True)
    i1 = jnp.min(jnp.where(logits == m1, lane, LANES), axis=-1, keepdims=True)
    first = lane == i1
    rest = jnp.where(first, -jnp.inf, logits)
    m2 = jnp.max(rest, axis=-1, keepdims=True)
    i2 = jnp.min(jnp.where(rest == m2, lane, LANES), axis=-1, keepdims=True)
    second = lane == i2
    e2 = jnp.exp(m2 - m1)
    denom = 1.0 + e2
    return jnp.where(first, 1.0 / denom, 0.0) + jnp.where(second, e2 / denom, 0.0)


def _moe_kernel(h_ref, g_ref, wr_ref, br_ref, wg_ref, wu_ref, wd_ref, fg_ref,
                o_ref, xn_ref, acc_ref, grow_ref, gcol_ref, rrow_ref, rcol_ref, *, final_norm):
    e = pl.program_id(1)
    tm, d = h_ref.shape

    @pl.when(e == 0)
    def _():
        x = h_ref[...]
        xn = _rms(x, g_ref[...])
        acc_ref[...] = x
        x_hi, x_lo = _split_bf16(xn)
        w_hi, w_lo = _split_bf16(wr_ref[...])
        xn_ref[...] = x_hi
        logits = _dot(x_hi, w_hi) + _dot(x_hi, w_lo) + _dot(x_lo, w_hi) + br_ref[...]
        lane = lax.broadcasted_iota(jnp.int32, (tm, LANES), 1)
        gcol = _top2_gates(jnp.where(lane < N_EXPERTS, logits, -jnp.inf), lane)
        gcol_ref[...] = gcol
        grow = gcol.T[:N_EXPERTS]
        grow_ref[...] = grow
        r = lax.broadcasted_iota(jnp.int32, (tm, tm), 0)
        c = lax.broadcasted_iota(jnp.int32, (tm, tm), 1)
        rrow_ref[...] = _dot(_one_hot(grow != 0.0), _one_hot(r < c))
        rcol_ref[...] = _dot(_one_hot(c < r), _one_hot(gcol != 0.0))

    gate_row = grow_ref[pl.ds(e, 1), :]
    rank_row = rrow_ref[pl.ds(e, 1), :]
    lane = lax.broadcasted_iota(jnp.int32, (tm, LANES), 1)
    gate_col = jnp.sum(jnp.where(lane == e, gcol_ref[...], 0.0), axis=-1, keepdims=True)
    rank_col = jnp.sum(jnp.where(lane == e, rcol_ref[...], 0.0), axis=-1, keepdims=True)
    count = jnp.sum(jnp.where(gate_row != 0.0, 1.0, 0.0)).astype(jnp.int32)

    def expert_block(cs, first_slot):
        off = jnp.full((1, 1), first_slot, jnp.int32).astype(F32)
        slot_s = lax.broadcasted_iota(jnp.int32, (cs, tm), 0).astype(F32)
        gather = _one_hot(jnp.logical_and(rank_row - off == slot_s, gate_row != 0.0))
        xg = _dot(gather, xn_ref[...]).astype(BF16)
        hid = (_silu(_dot(xg, wg_ref[0, 0])) * _dot(xg, wu_ref[0, 0])).astype(BF16)
        y = _dot(hid, wd_ref[0, 0]).astype(BF16)
        ks = -(-cs // LANES) * LANES
        if ks != cs:
            y = jnp.concatenate([y, jnp.zeros((ks - cs, d), BF16)], axis=0)
        slot_l = lax.broadcasted_iota(jnp.int32, (tm, ks), 1).astype(F32)
        scatter = _one_hot(jnp.logical_and(rank_col - off == slot_l, gate_col != 0.0))
        acc_ref[...] += gate_col * _dot(scatter, y)

    lower = 0
    for cs in MOE_SLOT_VARIANTS:
        @pl.when(jnp.logical_and(count > lower, count <= cs))
        def _(cs=cs):
            expert_block(cs, 0)
        lower = cs

    @pl.when(count > lower)
    def _():
        def body(b, carry):
            expert_block(MOE_SLOT_BLOCK, b * MOE_SLOT_BLOCK)
            return carry
        lax.fori_loop(0, (count + MOE_SLOT_BLOCK - 1) // MOE_SLOT_BLOCK, body, 0)

    @pl.when(e == pl.num_programs(1) - 1)
    def _():
        out = acc_ref[...]
        if final_norm:
            out = _rms(out, fg_ref[...])
        o_ref[...] = out


def _moe_layer(h, norm_g, w_router, b_router, w_gate, w_up, w_down, c, final_g, tm=1024):
    t, d = h.shape
    f = w_gate.shape[-1]
    final_norm = final_g is not None
    fg = (final_g if final_norm else jnp.ones((d,), F32)).reshape(1, d)
    wr = jnp.pad(w_router, ((0, 0), (0, LANES - N_EXPERTS)))
    br = jnp.pad(b_router, (0, LANES - N_EXPERTS)).reshape(1, LANES)
    return pl.pallas_call(
        functools.partial(_moe_kernel, final_norm=final_norm),
        grid=(t // tm, N_EXPERTS),
        in_specs=[
            pl.BlockSpec((tm, d), lambda i, e: (i, 0)),
            _resident((1, d)),
            _resident((d, LANES)),
            _resident((1, LANES)),
            pl.BlockSpec((1, 1, d, f), lambda i, e: (c, e, 0, 0)),
            pl.BlockSpec((1, 1, d, f), lambda i, e: (c, e, 0, 0)),
            pl.BlockSpec((1, 1, f, d), lambda i, e: (c, e, 0, 0)),
            _resident((1, d)),
        ],
        out_specs=pl.BlockSpec((tm, d), lambda i, e: (i, 0)),
        out_shape=jax.ShapeDtypeStruct((t, d), F32),
        scratch_shapes=[
            pltpu.VMEM((tm, d), BF16),
            pltpu.VMEM((tm, d), F32),
            pltpu.VMEM((N_EXPERTS, tm), F32),
            pltpu.VMEM((tm, LANES), F32),
            pltpu.VMEM((N_EXPERTS, tm), F32),
            pltpu.VMEM((tm, LANES), F32),
        ],
        compiler_params=_params(("arbitrary", "arbitrary")),
        name="moe_layer",
    )(h, norm_g.reshape(1, d), wr, br, w_gate, w_up, w_down, fg)


def _mla_proj_kernel(h_ref, g_ref, pos_ref, freq_ref, win_ref, qg_ref, wuq_ref, kvg_ref, wukv_ref,
                     kv_ref, q_out, k_out, v_out, memo_out):
    xn = _rms(h_ref[...], g_ref[...]).astype(BF16)
    proj = _dot(xn, win_ref[...])
    c_q = proj[:, :MLA_Q_LORA]
    c_kv = proj[:, MLA_Q_LORA:MLA_Q_LORA + MLA_KV_LORA]
    lo = MLA_Q_LORA + MLA_KV_LORA
    mem_q = proj[:, lo:lo + MEM_WIDTH]
    k_rope = proj[:, lo + MEM_WIDTH:lo + MEM_WIDTH + LANES]
    memo_out[...] = _mem_attention(mem_q, kv_ref).astype(BF16)

    ang = pos_ref[...].astype(F32) * freq_ref[...]
    lane = lax.broadcasted_iota(jnp.int32, ang.shape, 1)
    cos = jnp.where(lane < MLA_ROPE, jnp.cos(ang), 0.0)
    sin = jnp.sin(ang)
    sin = jnp.where(lane < MLA_ROPE // 2, -sin, jnp.where(lane < MLA_ROPE, sin, 0.0))

    def rope(r):
        return r * cos + pltpu.roll(r, MLA_ROPE // 2, 1) * sin

    scale = (MLA_NOPE + MLA_ROPE) ** -0.5
    q = _dot(_rms(c_q, qg_ref[...]).astype(BF16), wuq_ref[...])
    kvu = _dot(_rms(c_kv, kvg_ref[...]).astype(BF16), wukv_ref[...])
    k_r = rope(k_rope).astype(BF16)
    for hd in range(MLA_HEADS):
        qlo = hd * MLA_QK_PAD
        q_out[:, qlo:qlo + MLA_NOPE] = (q[:, qlo:qlo + MLA_NOPE] * scale).astype(BF16)
        q_out[:, qlo + MLA_NOPE:qlo + MLA_QK_PAD] = (
            rope(q[:, qlo + MLA_NOPE:qlo + MLA_QK_PAD]) * scale).astype(BF16)
        k_out[:, qlo:qlo + MLA_NOPE] = kvu[:, hd * MLA_NOPE:(hd + 1) * MLA_NOPE].astype(BF16)
        k_out[:, qlo + MLA_NOPE:qlo + MLA_QK_PAD] = k_r
    v_out[...] = kvu[:, MLA_HEADS * MLA_NOPE:].astype(BF16)


def _flash_kernel(qi_ref, kj_ref, q_ref, k_ref, v_ref, o_ref, m_ref, acc_ref):
    step_id = pl.program_id(1)
    i = qi_ref[step_id]
    j = kj_ref[step_id]
    tq, tk = q_ref.shape[0], k_ref.shape[0]

    @pl.when(j == 0)
    def _():
        m_ref[...] = jnp.full_like(m_ref, -jnp.inf)
        acc_ref[...] = jnp.zeros_like(acc_ref)

    ones = jnp.ones((tk, MLA_V), BF16)

    def step(masked):
        if masked:
            row = lax.broadcasted_iota(jnp.int32, (tq, tk), 0)
            col = lax.broadcasted_iota(jnp.int32, (tq, tk), 1)
            keep = col <= row
        for hd in range(MLA_HEADS):
            q = q_ref[:, hd * MLA_QK_PAD:(hd + 1) * MLA_QK_PAD]
            k = k_ref[:, hd * MLA_QK_PAD:(hd + 1) * MLA_QK_PAD]
            v_aug = jnp.concatenate([v_ref[:, hd * MLA_V:(hd + 1) * MLA_V], ones], axis=1)
            s = _dot_nt(q, k)
            if masked:
                s = jnp.where(keep, s, -jnp.inf)
            m_prev = m_ref[hd]
            m_new = jnp.maximum(m_prev, jnp.max(s, axis=-1, keepdims=True))
            alpha = jnp.exp(m_prev - m_new)
            p = jnp.exp((s - jnp.tile(m_new, (1, tk // LANES))).astype(BF16))
            acc_ref[hd] = jnp.tile(alpha, (1, 2)) * acc_ref[hd] + _dot(p, v_aug)
            m_ref[hd] = m_new

    @pl.when(j < i)
    def _():
        step(False)

    @pl.when(j == i)
    def _():
        step(True)
        for hd in range(MLA_HEADS):
            acc = acc_ref[hd]
            o_ref[:, hd * MLA_V:(hd + 1) * MLA_V] = (acc[:, :MLA_V] / acc[:, MLA_V:]).astype(BF16)


def _outproj_kernel(h_ref, a_ref, m_ref, wa_ref, wm_ref, o_ref):
    o_ref[...] = h_ref[...] + _dot(a_ref[...], wa_ref[...]) + _dot(m_ref[...], wm_ref[...])


def _mla_layer(h, batch, seq, layer, positions, norm_g, w_in, q_norm, w_uq, kv_norm, w_ukv,
               kv_all, w_out, tm=256, tq=512):
    t, d = h.shape
    half = MLA_ROPE // 2
    lo = MLA_Q_LORA + MLA_KV_LORA
    w_kr = w_in[:, lo:lo + MLA_ROPE]
    w_in_r = jnp.concatenate([w_in[:, :lo], w_in[:, lo + MLA_ROPE:], w_kr, w_kr], axis=1).astype(BF16)
    wq = w_uq.reshape(MLA_Q_LORA, MLA_HEADS, MLA_NOPE + MLA_ROPE)
    wq_r = jnp.concatenate([wq, wq[:, :, MLA_NOPE:]], axis=2).reshape(
        MLA_Q_LORA, MLA_HEADS * MLA_QK_PAD).astype(BF16)
    wkv = w_ukv.reshape(MLA_KV_LORA, MLA_HEADS, MLA_NOPE + MLA_V)
    wkv_r = jnp.concatenate([wkv[:, :, :MLA_NOPE].reshape(MLA_KV_LORA, -1),
                             wkv[:, :, MLA_NOPE:].reshape(MLA_KV_LORA, -1)], axis=1).astype(BF16)
    inv_freq = 1.0 / (ROPE_THETA ** (jnp.arange(0, MLA_ROPE, 2, dtype=F32) / MLA_ROPE))
    freq = jnp.tile(inv_freq, LANES // half).reshape(1, LANES)
    n_in = w_in_r.shape[1]
    tiles_per_seq = seq // tm
    m, kvw = kv_all.shape[2], kv_all.shape[3]
    qk_w = MLA_HEADS * MLA_QK_PAD
    v_w = MLA_HEADS * MLA_V
    q, k, v, memo = pl.pallas_call(
        _mla_proj_kernel,
        grid=(t // tm,),
        in_specs=[
            pl.BlockSpec((tm, d), lambda i: (i, 0)),
            _resident((1, d)),
            pl.BlockSpec((tm, 1), lambda i: (i, 0)),
            _resident((1, LANES)),
            _resident((d, n_in)),
            _resident((1, MLA_Q_LORA)),
            _resident((MLA_Q_LORA, qk_w)),
            _resident((1, MLA_KV_LORA)),
            _resident((MLA_KV_LORA, 2 * v_w)),
            pl.BlockSpec((1, 1, m, kvw), lambda i: (layer, i // tiles_per_seq, 0, 0)),
        ],
        out_specs=[
            pl.BlockSpec((tm, qk_w), lambda i: (i, 0)),
            pl.BlockSpec((tm, qk_w), lambda i: (i, 0)),
            pl.BlockSpec((tm, v_w), lambda i: (i, 0)),
            pl.BlockSpec((tm, MEM_WIDTH), lambda i: (i, 0)),
        ],
        out_shape=[
            jax.ShapeDtypeStruct((t, qk_w), BF16),
            jax.ShapeDtypeStruct((t, qk_w), BF16),
            jax.ShapeDtypeStruct((t, v_w), BF16),
            jax.ShapeDtypeStruct((t, MEM_WIDTH), BF16),
        ],
        compiler_params=_params(("arbitrary",)),
        name="mla_proj",
    )(h, norm_g.reshape(1, d), positions.reshape(t, 1), freq, w_in_r, q_norm.reshape(1, -1), wq_r,
      kv_norm.reshape(1, -1), wkv_r, kv_all)

    nq = seq // tq
    pairs = [(i, j) for i in range(nq) for j in range(i + 1)]
    qi = jnp.array([p[0] for p in pairs], jnp.int32)
    kj = jnp.array([p[1] for p in pairs], jnp.int32)
    attn = pl.pallas_call(
        _flash_kernel,
        grid_spec=pltpu.PrefetchScalarGridSpec(
            num_scalar_prefetch=2,
            grid=(batch, len(pairs)),
            in_specs=[
                pl.BlockSpec((tq, qk_w), lambda b, s, qi, kj: (b * nq + qi[s], 0)),
                pl.BlockSpec((tq, qk_w), lambda b, s, qi, kj: (b * nq + kj[s], 0)),
                pl.BlockSpec((tq, v_w), lambda b, s, qi, kj: (b * nq + kj[s], 0)),
            ],
            out_specs=pl.BlockSpec((tq, v_w), lambda b, s, qi, kj: (b * nq + qi[s], 0)),
            scratch_shapes=[
                pltpu.VMEM((MLA_HEADS, tq, LANES), F32),
                pltpu.VMEM((MLA_HEADS, tq, 2 * MLA_V), F32),
            ],
        ),
        out_shape=jax.ShapeDtypeStruct((t, v_w), BF16),
        compiler_params=_params(("arbitrary", "arbitrary")),
        name="mla_flash",
    )(qi, kj, q, k, v)

    to = 512
    w_out_b = w_out.astype(BF16)
    return pl.pallas_call(
        _outproj_kernel,
        grid=(t // to,),
        in_specs=[
            pl.BlockSpec((to, d), lambda i: (i, 0)),
            pl.BlockSpec((to, v_w), lambda i: (i, 0)),
            pl.BlockSpec((to, MEM_WIDTH), lambda i: (i, 0)),
            _resident((v_w, d)),
            _resident((MEM_WIDTH, d)),
        ],
        out_specs=pl.BlockSpec((to, d), lambda i: (i, 0)),
        out_shape=jax.ShapeDtypeStruct((t, d), F32),
        compiler_params=_params(("arbitrary",)),
        name="mla_outproj",
    )(h, attn, memo, w_out_b[:v_w], w_out_b[v_w:])


def _mlstm_layer_kernel(h_ref, g_ref, win_ref, cw_ref, cb_ref, gb_ref, hg_ref, kv_ref, wout_ref,
                        o_ref, qk_ref, c_ref, m_ref, mix_ref):
    tm = h_ref.shape[0]
    qk_w = 2 * ML_HEADS * ML_DK
    hv = ML_HEADS * ML_DV
    pad = 8

    @pl.when(pl.program_id(1) == 0)
    def _():
        qk_ref[0:pad, :] = jnp.zeros((pad, qk_w), F32)
        c_ref[...] = jnp.zeros_like(c_ref)
        m_ref[...] = jnp.zeros_like(m_ref)

    x = h_ref[...]
    xn = _rms(x, g_ref[...]).astype(BF16)
    proj = _dot(xn, win_ref[...])
    qk_ref[pad:pad + tm, :] = proj[:, :qk_w]
    conv = cb_ref[...]
    for j in range(CONV_K):
        conv = conv + cw_ref[j:j + 1, :] * qk_ref[pad - (CONV_K - 1) + j:pad - (CONV_K - 1) + j + tm, :]
    qk_ref[0:pad, :] = qk_ref[tm:tm + pad, :]
    qk = _silu(conv)
    v_all = proj[:, qk_w:qk_w + hv]
    o_all = proj[:, qk_w + hv:qk_w + 2 * hv]
    mem_q = proj[:, qk_w + 2 * hv:qk_w + 2 * hv + MEM_WIDTH]
    gates = proj[:, qk_w + 2 * hv + MEM_WIDTH:] + gb_ref[...]

    lane = lax.broadcasted_iota(jnp.int32, (CHUNK, LANES), 1)
    row = lax.broadcasted_iota(jnp.int32, (CHUNK, CHUNK), 0)
    col = lax.broadcasted_iota(jnp.int32, (CHUNK, CHUNK), 1)
    causal = col <= row
    tril = _one_hot(causal)
    ones_blk = _one_hot(lane == 0)

    for c in range(tm // CHUNK):
        r0 = c * CHUNK
        gc = gates[r0:r0 + CHUNK]
        logs = jnp.where(lane >= ML_HEADS, jax.nn.log_sigmoid(gc), gc)
        hi = logs.astype(BF16)
        r1 = logs - hi.astype(F32)
        mid = r1.astype(BF16)
        low = (r1 - mid.astype(F32)).astype(BF16)
        cum = _dot(tril, hi) + _dot(tril, mid) + _dot(tril, low)
        col_vals = jnp.where(lane >= ML_HEADS, cum, logs)
        row_vals = col_vals.T
        for hd in range(ML_HEADS):
            q = qk[r0:r0 + CHUNK, hd * ML_DK:(hd + 1) * ML_DK].astype(BF16)
            k32 = qk[r0:r0 + CHUNK, qk_w // 2 + hd * ML_DK:qk_w // 2 + (hd + 1) * ML_DK] * ML_DK ** -0.5
            v = v_all[r0:r0 + CHUNK, hd * ML_DV:(hd + 1) * ML_DV].astype(BF16)
            v_aug = jnp.concatenate([v, ones_blk], axis=1)
            a_col = col_vals[:, ML_HEADS + hd:ML_HEADS + hd + 1]
            li_col = col_vals[:, hd:hd + 1]
            b_row = row_vals[ML_HEADS + hd:ML_HEADS + hd + 1, :]
            li_row = row_vals[hd:hd + 1, :]
            m_in = m_ref[hd]
            c_in = c_ref[hd]

            log_d = jnp.where(causal, a_col - b_row + li_row, -jnp.inf)
            m_intra = jnp.max(log_d, axis=-1, keepdims=True)
            log_inter = a_col + m_in
            m_t = jnp.maximum(log_inter, m_intra)
            inter = jnp.exp(log_inter - m_t)
            p = (_dot_nt(q, k32.astype(BF16)) * jnp.exp(log_d - m_t)).astype(BF16)
            numden = inter * _dot(q, c_in.astype(BF16)) + _dot(p, v_aug)
            den = numden[:, ML_DV:ML_DV + 1]
            h_out = numden[:, :ML_DV] / jnp.maximum(jnp.abs(den), jnp.exp(-m_t))
            h_n = _rms(h_out, hg_ref[:, hd * ML_DV:(hd + 1) * ML_DV])
            gate_o = jax.nn.sigmoid(o_all[r0:r0 + CHUNK, hd * ML_DV:(hd + 1) * ML_DV])
            mix_ref[r0:r0 + CHUNK, hd * ML_DV:(hd + 1) * ML_DV] = (h_n * gate_o).astype(BF16)

            f_tot = b_row[:, CHUNK - 1:CHUNK]
            log_w = f_tot - a_col + li_col
            m_loc = jnp.max(log_w, axis=0, keepdims=True)
            kw_t = (k32 * jnp.exp(log_w - m_loc)).T.astype(BF16)
            c_loc = _dot(kw_t, v_aug)
            m_new = jnp.maximum(f_tot + m_in, m_loc)
            c_ref[hd] = jnp.exp(f_tot + m_in - m_new) * c_in + jnp.exp(m_loc - m_new) * c_loc
            m_ref[hd] = m_new

    mix_ref[:, hv:hv + MEM_WIDTH] = _mem_attention(mem_q, kv_ref).astype(BF16)
    o_ref[...] = x + _dot(mix_ref[...], wout_ref[...])


def _mlstm_layer(h, batch, seq, layer, norm_g, w_in, conv_w, conv_b, gate_b, h_norm, kv_all, w_out,
                 tm=256):
    t, d = h.shape
    qk_w = 2 * ML_HEADS * ML_DK
    hv = ML_HEADS * ML_DV
    n_gate = 2 * ML_HEADS
    lo = qk_w + 2 * hv
    w_in_r = jnp.concatenate(
        [w_in[:, :lo], w_in[:, lo + n_gate:], w_in[:, lo:lo + n_gate],
         jnp.zeros((d, LANES - n_gate), w_in.dtype)], axis=1).astype(BF16)
    gb = jnp.pad(gate_b, (0, LANES - n_gate)).reshape(1, LANES)
    n_in = w_in_r.shape[1]
    n_mix = w_out.shape[0]
    tiles = seq // tm
    m, kvw = kv_all.shape[2], kv_all.shape[3]
    return pl.pallas_call(
        _mlstm_layer_kernel,
        grid=(batch, tiles),
        in_specs=[
            pl.BlockSpec((tm, d), lambda b, i: (b * tiles + i, 0)),
            _resident((1, d)),
            _resident((d, n_in)),
            _resident((CONV_K, qk_w)),
            _resident((1, qk_w)),
            _resident((1, LANES)),
            _resident((1, hv)),
            pl.BlockSpec((1, 1, m, kvw), lambda b, i: (layer, b, 0, 0)),
            _resident((n_mix, d)),
        ],
        out_specs=pl.BlockSpec((tm, d), lambda b, i: (b * tiles + i, 0)),
        out_shape=jax.ShapeDtypeStruct((t, d), F32),
        scratch_shapes=[
            pltpu.VMEM((tm + 8, qk_w), F32),
            pltpu.VMEM((ML_HEADS, ML_DK, ML_AUG), F32),
            pltpu.VMEM((ML_HEADS, 1, 1), F32),
            pltpu.VMEM((tm, n_mix), BF16),
        ],
        compiler_params=_params(("arbitrary", "arbitrary")),
        name="mlstm_layer",
    )(h, norm_g.reshape(1, d), w_in_r, conv_w, conv_b.reshape(1, qk_w), gb, h_norm.reshape(1, hv),
      kv_all, w_out.astype(BF16))


def kernel(x, mem, positions, attn_norm, mem_norm, w_mem_kv, ffn_norm, final_norm, gm_w_in, gm_v_norm, gm_w_s, gm_b_s, gm_w_out, mla_w_in, mla_q_norm, mla_w_uq, mla_kv_norm, mla_w_ukv, mla_w_out, ml_w_in, ml_conv_w, ml_conv_b, ml_gate_b, ml_h_norm, ml_w_out, ff_w_gate, ff_w_up, ff_w_down, moe_w_router, moe_b_router, moe_w_gate, moe_w_up, moe_w_down):
    batch, seq, d = x.shape
    depth = attn_norm.shape[0]
    kv_all = _memkv(mem, mem_norm, w_mem_kv)
    ff_w = [w.astype(BF16) for w in (ff_w_gate, ff_w_up, ff_w_down)]
    moe_w = [w.astype(BF16) for w in (moe_w_gate, moe_w_up, moe_w_down)]
    h = x.reshape(batch * seq, d)
    for layer in range(depth):
        kind, j = layer % 3, layer // 3
        if kind == 0:
            h = _gmlp_layer(h, seq, layer, attn_norm[layer], gm_w_in[j], gm_v_norm[j], gm_w_s[j],
                            gm_b_s[j], kv_all, gm_w_out[j])
        elif kind == 1:
            h = _mla_layer(h, batch, seq, layer, positions, attn_norm[layer], mla_w_in[j],
                           mla_q_norm[j], mla_w_uq[j], mla_kv_norm[j], mla_w_ukv[j], kv_all,
                           mla_w_out[j])
        else:
            h = _mlstm_layer(h, batch, seq, layer, attn_norm[layer], ml_w_in[j], ml_conv_w[j],
                             ml_conv_b[j], ml_gate_b[j], ml_h_norm[j], kv_all, ml_w_out[j])
        c = layer // 2
        last = layer == depth - 1
        if layer % 2 == 0:
            h = _ffn_layer(h, ffn_norm[layer], *ff_w, c)
            if last:
                h = _final_norm(h, final_norm)
        else:
            h = _moe_layer(h, ffn_norm[layer], moe_w_router[c], moe_b_router[c], *moe_w, c,
                           final_norm if last else None)
    return h.reshape(batch, seq, d)


def _final_norm_kernel(h_ref, g_ref, o_ref):
    o_ref[...] = _rms(h_ref[...], g_ref[...])


def _final_norm(h, g, tm=1024):
    t, d = h.shape
    return pl.pallas_call(
        _final_norm_kernel,
        grid=(t // tm,),
        in_specs=[pl.BlockSpec((tm, d), lambda i: (i, 0)), _resident((1, d))],
        out_specs=pl.BlockSpec((tm, d), lambda i: (i, 0)),
        out_shape=jax.ShapeDtypeStruct((t, d), F32),
        compiler_params=_params(("arbitrary",)),
        name="final_norm",
    )(h, g.reshape(1, d))
```

```python
import functools

import jax
import jax.numpy as jnp
from jax import lax
from jax.experimental import pallas as pl
from jax.experimental.pallas import tpu as pltpu

F32 = jnp.float32
BF16 = jnp.bfloat16

NORM_EPS = 1e-6
CHUNK = 128
LANES = 128
MEM_HEADS = 4
MEM_HEAD_DIM = 128
MEM_WIDTH = MEM_HEADS * MEM_HEAD_DIM
GM_GROUPS = 8
GM_GROUP_DIM = 128
GM_WIDTH = GM_GROUPS * GM_GROUP_DIM
MLA_HEADS = 8
MLA_NOPE = 128
MLA_ROPE = 64
MLA_V = 128
MLA_Q_LORA = 768
MLA_KV_LORA = 256
MLA_QK_PAD = 256
ROPE_THETA = 10000.0
ML_HEADS = 4
ML_DV = 256
ML_DK = 128
ML_AUG = ML_DV + LANES
CONV_K = 4
N_EXPERTS = 8
TOP_K = 2

VMEM_LIMIT = 56 * 1024 * 1024

NT_DIMS = (((1,), (1,)), ((), ()))


def _params(semantics, vmem=VMEM_LIMIT):
    return pltpu.CompilerParams(dimension_semantics=semantics, vmem_limit_bytes=vmem)


def _resident(shape, index=None):
    index = (0,) * len(shape) if index is None else index
    return pl.BlockSpec(shape, lambda *_: index, pipeline_mode=pl.Buffered(1))


def _dot(a, b):
    return jnp.dot(a, b, preferred_element_type=F32)


def _dot_nt(a, b):
    return lax.dot_general(a, b, NT_DIMS, preferred_element_type=F32)


def _rms(x, g):
    return x * lax.rsqrt(jnp.mean(x * x, axis=-1, keepdims=True) + NORM_EPS) * g


def _silu(x):
    return x * jax.nn.sigmoid(x)


def _gelu(x):
    return 0.5 * x * (1.0 + lax.erf(x * 0.5 ** 0.5))


def _one_hot(mask):
    return jnp.where(mask, 1.0, 0.0).astype(BF16)


def _memkv_kernel(mem_ref, g_ref, w_ref, kv_ref):
    xn = _rms(mem_ref[0], g_ref[0]).astype(BF16)
    kv_ref[0, 0] = _dot(xn, w_ref[0]).astype(BF16)


def _memkv(mem, mem_norm, w_mem_kv):
    b, m, d = mem.shape
    depth = mem_norm.shape[0]
    n = w_mem_kv.shape[-1]
    return pl.pallas_call(
        _memkv_kernel,
        grid=(depth, b),
        in_specs=[
            pl.BlockSpec((1, m, d), lambda l, i: (i, 0, 0)),
            pl.BlockSpec((1, 1, d), lambda l, i: (l, 0, 0)),
            pl.BlockSpec((1, d, n), lambda l, i: (l, 0, 0)),
        ],
        out_specs=pl.BlockSpec((1, 1, m, n), lambda l, i: (l, i, 0, 0)),
        out_shape=jax.ShapeDtypeStruct((depth, b, m, n), BF16),
        compiler_params=_params(("arbitrary", "arbitrary")),
        name="memkv",
    )(mem, mem_norm.reshape(depth, 1, d), w_mem_kv.astype(BF16))


def _mem_attention(q, kv_ref):
    outs = []
    for hd in range(MEM_HEADS):
        lo = hd * MEM_HEAD_DIM
        qh = q[:, lo:lo + MEM_HEAD_DIM].astype(BF16)
        kh = kv_ref[0, 0, :, lo:lo + MEM_HEAD_DIM]
        vh = kv_ref[0, 0, :, MEM_WIDTH + lo:MEM_WIDTH + lo + MEM_HEAD_DIM]
        s = _dot_nt(qh, kh) * MEM_HEAD_DIM ** -0.5
        p = jnp.exp(s - jnp.max(s, axis=-1, keepdims=True))
        outs.append(_dot(p.astype(BF16), vh) / jnp.sum(p, axis=-1, keepdims=True))
    return jnp.concatenate(outs, axis=-1)


def _gmlp_layer_kernel(h_ref, g_ref, win_ref, vg_ref, ws_ref, bs_ref, kv_ref, wout_ref,
                       o_ref, proj_ref, mix_ref):
    tm = h_ref.shape[0]
    x = h_ref[...]
    xn = _rms(x, g_ref[...]).astype(BF16)
    proj_ref[...] = _dot(xn, win_ref[...])
    row = lax.broadcasted_iota(jnp.int32, (CHUNK, CHUNK), 0)
    col = lax.broadcasted_iota(jnp.int32, (CHUNK, CHUNK), 1)
    causal = col <= row
    for g in range(GM_GROUPS):
        lo = g * GM_GROUP_DIM
        u = _gelu(proj_ref[:, lo:lo + GM_GROUP_DIM])
        v = _gelu(proj_ref[:, GM_WIDTH + lo:GM_WIDTH + lo + GM_GROUP_DIM])
        vn = _rms(v, vg_ref[:, lo:lo + GM_GROUP_DIM]).astype(BF16)
        w = jnp.where(causal, ws_ref[g], 0.0).astype(BF16)
        for c in range(tm // CHUNK):
            r = c * CHUNK
            mixed = _dot(w, vn[r:r + CHUNK]) + bs_ref[g]
            mix_ref[r:r + CHUNK, lo:lo + GM_GROUP_DIM] = (u[r:r + CHUNK] * mixed).astype(BF16)
    mem_o = _mem_attention(proj_ref[:, 2 * GM_WIDTH:2 * GM_WIDTH + MEM_WIDTH], kv_ref)
    mix_ref[:, GM_WIDTH:GM_WIDTH + MEM_WIDTH] = mem_o.astype(BF16)
    o_ref[...] = x + _dot(mix_ref[...], wout_ref[...])


def _gmlp_layer(h, seq, layer, norm_g, w_in, v_norm, w_s, b_s, kv_all, w_out, tm=512):
    t, d = h.shape
    n_in = w_in.shape[1]
    n_mix = w_out.shape[0]
    tiles_per_seq = seq // tm
    m, kvw = kv_all.shape[2], kv_all.shape[3]
    bs_full = jnp.broadcast_to(b_s[:, :, None], (GM_GROUPS, CHUNK, GM_GROUP_DIM)).astype(F32)
    return pl.pallas_call(
        _gmlp_layer_kernel,
        grid=(t // tm,),
        in_specs=[
            pl.BlockSpec((tm, d), lambda i: (i, 0)),
            _resident((1, d)),
            _resident((d, n_in)),
            _resident((1, GM_WIDTH)),
            _resident((GM_GROUPS, CHUNK, CHUNK)),
            _resident((GM_GROUPS, CHUNK, GM_GROUP_DIM)),
            pl.BlockSpec((1, 1, m, kvw), lambda i: (layer, i // tiles_per_seq, 0, 0)),
            _resident((n_mix, d)),
        ],
        out_specs=pl.BlockSpec((tm, d), lambda i: (i, 0)),
        out_shape=jax.ShapeDtypeStruct((t, d), F32),
        scratch_shapes=[pltpu.VMEM((tm, n_in), F32), pltpu.VMEM((tm, n_mix), BF16)],
        compiler_params=_params(("arbitrary",)),
        name="gmlp_layer",
    )(h, norm_g.reshape(1, d), w_in.astype(BF16), v_norm.reshape(1, GM_WIDTH), w_s,
      bs_full, kv_all, w_out.astype(BF16))


def _ffn_kernel(h_ref, g_ref, wg_ref, wu_ref, wd_ref, o_ref, hid_ref, *, fc):
    x = h_ref[...]
    xn = _rms(x, g_ref[...]).astype(BF16)
    f = wg_ref.shape[-1]
    for c in range(f // fc):
        lo = c * fc
        gate = _dot(xn, wg_ref[0, :, lo:lo + fc])
        up = _dot(xn, wu_ref[0, :, lo:lo + fc])
        hid_ref[:, lo:lo + fc] = (_silu(gate) * up).astype(BF16)
    o_ref[...] = x + _dot(hid_ref[...], wd_ref[0])


def _ffn_layer(h, norm_g, w_gate, w_up, w_down, c, tm=512, fc=256):
    t, d = h.shape
    f = w_gate.shape[-1]
    return pl.pallas_call(
        functools.partial(_ffn_kernel, fc=fc),
        grid=(t // tm,),
        in_specs=[
            pl.BlockSpec((tm, d), lambda i: (i, 0)),
            _resident((1, d)),
            _resident((1, d, f), (c, 0, 0)),
            _resident((1, d, f), (c, 0, 0)),
            _resident((1, f, d), (c, 0, 0)),
        ],
        out_specs=pl.BlockSpec((tm, d), lambda i: (i, 0)),
        out_shape=jax.ShapeDtypeStruct((t, d), F32),
        scratch_shapes=[pltpu.VMEM((tm, f), BF16)],
        compiler_params=_params(("arbitrary",)),
        name="ffn_layer",
    )(h, norm_g.reshape(1, d), w_gate, w_up, w_down)


MOE_SLOT_BLOCK = 256
MOE_SLOT_VARIANTS = (256, 288, 320)


def _split_bf16(x):
    hi = x.astype(BF16)
    return hi, (x - hi.astype(F32)).astype(BF16)


def _top2_gates(logits, lane):
    m1 = jnp.max(logits, axis=-1, keepdims=True)
    i1 = jnp.min(jnp.where(logits == m1, lane, LANES), axis=-1, keepdims=True)
    first = lane == i1
    rest = jnp.where(first, -jnp.inf, logits)
    m2 = jnp.max(rest, axis=-1, keepdims=True)
    i2 = jnp.min(jnp.where(rest == m2, lane, LANES), axis=-1, keepdims=True)
    second = lane == i2
    e2 = jnp.exp(m2 - m1)
    denom = 1.0 + e2
    return jnp.where(first, 1.0 / denom, 0.0) + jnp.where(second, e2 / denom, 0.0)


def _moe_kernel(h_ref, g_ref, wr_ref, br_ref, wg_ref, wu_ref, wd_ref, fg_ref,
                o_ref, xn_ref, acc_ref, grow_ref, gcol_ref, rrow_ref, rcol_ref, *, final_norm):
    e = pl.program_id(1)
    tm, d = h_ref.shape

    @pl.when(e == 0)
    def _():
        x = h_ref[...]
        xn = _rms(x, g_ref[...])
        acc_ref[...] = x
        x_hi, x_lo = _split_bf16(xn)
        w_hi, w_lo = _split_bf16(wr_ref[...])
        xn_ref[...] = x_hi
        logits = _dot(x_hi, w_hi) + _dot(x_hi, w_lo) + _dot(x_lo, w_hi) + br_ref[...]
        lane = lax.broadcasted_iota(jnp.int32, (tm, LANES), 1)
        gcol = _top2_gates(jnp.where(lane < N_EXPERTS, logits, -jnp.inf), lane)
        gcol_ref[...] = gcol
        grow_ref[...] = gcol.T[:N_EXPERTS]
        sel = _one_hot(gcol != 0.0)
        r = lax.broadcasted_iota(jnp.int32, (CHUNK, CHUNK), 0)
        c = lax.broadcasted_iota(jnp.int32, (CHUNK, CHUNK), 1)
        earlier = _one_hot(c < r)
        base = jnp.zeros((1, LANES), F32)
        for b in range(tm // CHUNK):
            blk = sel[b * CHUNK:(b + 1) * CHUNK]
            rcol_ref[b * CHUNK:(b + 1) * CHUNK, :] = _dot(earlier, blk) + base
            base = base + jnp.sum(blk.astype(F32), axis=0, keepdims=True)
        rrow_ref[...] = rcol_ref[...].T[:N_EXPERTS]

    gate_row = grow_ref[pl.ds(e, 1), :]
    rank_row = rrow_ref[pl.ds(e, 1), :]
    lane = lax.broadcasted_iota(jnp.int32, (tm, LANES), 1)
    gate_col = jnp.sum(jnp.where(lane == e, gcol_ref[...], 0.0), axis=-1, keepdims=True)
    rank_col = jnp.sum(jnp.where(lane == e, rcol_ref[...], 0.0), axis=-1, keepdims=True)
    count = jnp.sum(jnp.where(gate_row != 0.0, 1.0, 0.0)).astype(jnp.int32)

    def expert_block(cs, first_slot):
        off = jnp.full((1, 1), first_slot, jnp.int32).astype(F32)
        slot_s = lax.broadcasted_iota(jnp.int32, (cs, tm), 0).astype(F32)
        gather = _one_hot(jnp.logical_and(rank_row - off == slot_s, gate_row != 0.0))
        xg = _dot(gather, xn_ref[...]).astype(BF16)
        hid = (_silu(_dot(xg, wg_ref[0, 0])) * _dot(xg, wu_ref[0, 0])).astype(BF16)
        y = _dot(hid, wd_ref[0, 0]).astype(BF16)
        ks = -(-cs // LANES) * LANES
        if ks != cs:
            y = jnp.concatenate([y, jnp.zeros((ks - cs, d), BF16)], axis=0)
        slot_l = lax.broadcasted_iota(jnp.int32, (tm, ks), 1).astype(F32)
        scatter = _one_hot(jnp.logical_and(rank_col - off == slot_l, gate_col != 0.0))
        acc_ref[...] += gate_col * _dot(scatter, y)

    lower = 0
    for cs in MOE_SLOT_VARIANTS:
        @pl.when(jnp.logical_and(count > lower, count <= cs))
        def _(cs=cs):
            expert_block(cs, 0)
        lower = cs

    @pl.when(count > lower)
    def _():
        def body(b, carry):
            expert_block(MOE_SLOT_BLOCK, b * MOE_SLOT_BLOCK)
            return carry
        lax.fori_loop(0, (count + MOE_SLOT_BLOCK - 1) // MOE_SLOT_BLOCK, body, 0)

    @pl.when(e == pl.num_programs(1) - 1)
    def _():
        out = acc_ref[...]
        if final_norm:
            out = _rms(out, fg_ref[...])
        o_ref[...] = out


def _moe_layer(h, norm_g, w_router, b_router, w_gate, w_up, w_down, c, final_g, tm=1024):
    t, d = h.shape
    f = w_gate.shape[-1]
    final_norm = final_g is not None
    fg = (final_g if final_norm else jnp.ones((d,), F32)).reshape(1, d)
    wr = jnp.pad(w_router, ((0, 0), (0, LANES - N_EXPERTS)))
    br = jnp.pad(b_router, (0, LANES - N_EXPERTS)).reshape(1, LANES)
    return pl.pallas_call(
        functools.partial(_moe_kernel, final_norm=final_norm),
        grid=(t // tm, N_EXPERTS),
        in_specs=[
            pl.BlockSpec((tm, d), lambda i, e: (i, 0)),
            _resident((1, d)),
            _resident((d, LANES)),
            _resident((1, LANES)),
            pl.BlockSpec((1, 1, d, f), lambda i, e: (c, e, 0, 0)),
            pl.BlockSpec((1, 1, d, f), lambda i, e: (c, e, 0, 0)),
            pl.BlockSpec((1, 1, f, d), lambda i, e: (c, e, 0, 0)),
            _resident((1, d)),
        ],
        out_specs=pl.BlockSpec((tm, d), lambda i, e: (i, 0)),
        out_shape=jax.ShapeDtypeStruct((t, d), F32),
        scratch_shapes=[
            pltpu.VMEM((tm, d), BF16),
            pltpu.VMEM((tm, d), F32),
            pltpu.VMEM((N_EXPERTS, tm), F32),
            pltpu.VMEM((tm, LANES), F32),
            pltpu.VMEM((N_EXPERTS, tm), F32),
            pltpu.VMEM((tm, LANES), F32),
        ],
        compiler_params=_params(("arbitrary", "arbitrary")),
        name="moe_layer",
    )(h, norm_g.reshape(1, d), wr, br, w_gate, w_up, w_down, fg)


def _mla_proj_kernel(h_ref, g_ref, pos_ref, freq_ref, win_ref, qg_ref, wuq_ref, kvg_ref, wukv_ref,
                     kv_ref, q_out, k_out, v_out, memo_out):
    xn = _rms(h_ref[...], g_ref[...]).astype(BF16)
    proj = _dot(xn, win_ref[...])
    c_q = proj[:, :MLA_Q_LORA]
    c_kv = proj[:, MLA_Q_LORA:MLA_Q_LORA + MLA_KV_LORA]
    lo = MLA_Q_LORA + MLA_KV_LORA
    mem_q = proj[:, lo:lo + MEM_WIDTH]
    k_rope = proj[:, lo + MEM_WIDTH:lo + MEM_WIDTH + LANES]
    memo_out[...] = _mem_attention(mem_q, kv_ref).astype(BF16)

    ang = pos_ref[...].astype(F32) * freq_ref[...]
    lane = lax.broadcasted_iota(jnp.int32, ang.shape, 1)
    cos = jnp.where(lane < MLA_ROPE, jnp.cos(ang), 0.0)
    sin = jnp.sin(ang)
    sin = jnp.where(lane < MLA_ROPE // 2, -sin, jnp.where(lane < MLA_ROPE, sin, 0.0))

    def rope(r):
        return r * cos + pltpu.roll(r, MLA_ROPE // 2, 1) * sin

    scale = (MLA_NOPE + MLA_ROPE) ** -0.5
    q = _dot(_rms(c_q, qg_ref[...]).astype(BF16), wuq_ref[...])
    kvu = _dot(_rms(c_kv, kvg_ref[...]).astype(BF16), wukv_ref[...])
    k_r = rope(k_rope).astype(BF16)
    for hd in range(MLA_HEADS):
        qlo = hd * MLA_QK_PAD
        q_out[:, qlo:qlo + MLA_NOPE] = (q[:, qlo:qlo + MLA_NOPE] * scale).astype(BF16)
        q_out[:, qlo + MLA_NOPE:qlo + MLA_QK_PAD] = (
            rope(q[:, qlo + MLA_NOPE:qlo + MLA_QK_PAD]) * scale).astype(BF16)
        k_out[:, qlo:qlo + MLA_NOPE] = kvu[:, hd * MLA_NOPE:(hd + 1) * MLA_NOPE].astype(BF16)
        k_out[:, qlo + MLA_NOPE:qlo + MLA_QK_PAD] = k_r
    v_out[...] = kvu[:, MLA_HEADS * MLA_NOPE:].astype(BF16)


def _flash_kernel(qi_ref, kj_ref, q_ref, k_ref, v_ref, h_ref, memo_ref, wa_ref, wm_ref, o_ref,
                  m_ref, acc_ref, attn_ref):
    step_id = pl.program_id(1)
    i = qi_ref[step_id]
    j = kj_ref[step_id]
    tq, tk = q_ref.shape[0], k_ref.shape[0]

    @pl.when(j == 0)
    def _():
        m_ref[...] = jnp.full_like(m_ref, -jnp.inf)
        acc_ref[...] = jnp.zeros_like(acc_ref)

    ones = jnp.ones((tk, MLA_V), BF16)

    def step(masked):
        if masked:
            row = lax.broadcasted_iota(jnp.int32, (tq, tk), 0)
            col = lax.broadcasted_iota(jnp.int32, (tq, tk), 1)
            keep = col <= row
        for hd in range(MLA_HEADS):
            q = q_ref[:, hd * MLA_QK_PAD:(hd + 1) * MLA_QK_PAD]
            k = k_ref[:, hd * MLA_QK_PAD:(hd + 1) * MLA_QK_PAD]
            v_aug = jnp.concatenate([v_ref[:, hd * MLA_V:(hd + 1) * MLA_V], ones], axis=1)
            s = _dot_nt(q, k)
            if masked:
                s = jnp.where(keep, s, -jnp.inf)
            m_prev = m_ref[hd]
            m_new = jnp.maximum(m_prev, jnp.max(s, axis=-1, keepdims=True))
            alpha = jnp.exp(m_prev - m_new)
            p = jnp.exp((s - jnp.tile(m_new, (1, tk // LANES))).astype(BF16))
            acc_ref[hd] = jnp.tile(alpha, (1, 2)) * acc_ref[hd] + _dot(p, v_aug)
            m_ref[hd] = m_new

    @pl.when(j < i)
    def _():
        step(False)

    @pl.when(j == i)
    def _():
        step(True)
        for hd in range(MLA_HEADS):
            acc = acc_ref[hd]
            attn_ref[:, hd * MLA_V:(hd + 1) * MLA_V] = (acc[:, :MLA_V] / acc[:, MLA_V:]).astype(BF16)
        o_ref[...] = (h_ref[...] + _dot(attn_ref[...], wa_ref[...])
                      + _dot(memo_ref[...], wm_ref[...]))


def _mla_layer(h, batch, seq, layer, positions, norm_g, w_in, q_norm, w_uq, kv_norm, w_ukv,
               kv_all, w_out, tm=512, tq=512):
    t, d = h.shape
    half = MLA_ROPE // 2
    lo = MLA_Q_LORA + MLA_KV_LORA
    w_kr = w_in[:, lo:lo + MLA_ROPE]
    w_in_r = jnp.concatenate([w_in[:, :lo], w_in[:, lo + MLA_ROPE:], w_kr, w_kr], axis=1).astype(BF16)
    wq = w_uq.reshape(MLA_Q_LORA, MLA_HEADS, MLA_NOPE + MLA_ROPE)
    wq_r = jnp.concatenate([wq, wq[:, :, MLA_NOPE:]], axis=2).reshape(
        MLA_Q_LORA, MLA_HEADS * MLA_QK_PAD).astype(BF16)
    wkv = w_ukv.reshape(MLA_KV_LORA, MLA_HEADS, MLA_NOPE + MLA_V)
    wkv_r = jnp.concatenate([wkv[:, :, :MLA_NOPE].reshape(MLA_KV_LORA, -1),
                             wkv[:, :, MLA_NOPE:].reshape(MLA_KV_LORA, -1)], axis=1).astype(BF16)
    inv_freq = 1.0 / (ROPE_THETA ** (jnp.arange(0, MLA_ROPE, 2, dtype=F32) / MLA_ROPE))
    freq = jnp.tile(inv_freq, LANES // half).reshape(1, LANES)
    n_in = w_in_r.shape[1]
    tiles_per_seq = seq // tm
    m, kvw = kv_all.shape[2], kv_all.shape[3]
    qk_w = MLA_HEADS * MLA_QK_PAD
    v_w = MLA_HEADS * MLA_V
    q, k, v, memo = pl.pallas_call(
        _mla_proj_kernel,
        grid=(t // tm,),
        in_specs=[
            pl.BlockSpec((tm, d), lambda i: (i, 0)),
            _resident((1, d)),
            pl.BlockSpec((tm, 1), lambda i: (i, 0)),
            _resident((1, LANES)),
            _resident((d, n_in)),
            _resident((1, MLA_Q_LORA)),
            _resident((MLA_Q_LORA, qk_w)),
            _resident((1, MLA_KV_LORA)),
            _resident((MLA_KV_LORA, 2 * v_w)),
            pl.BlockSpec((1, 1, m, kvw), lambda i: (layer, i // tiles_per_seq, 0, 0)),
        ],
        out_specs=[
            pl.BlockSpec((tm, qk_w), lambda i: (i, 0)),
            pl.BlockSpec((tm, qk_w), lambda i: (i, 0)),
            pl.BlockSpec((tm, v_w), lambda i: (i, 0)),
            pl.BlockSpec((tm, MEM_WIDTH), lambda i: (i, 0)),
        ],
        out_shape=[
            jax.ShapeDtypeStruct((t, qk_w), BF16),
            jax.ShapeDtypeStruct((t, qk_w), BF16),
            jax.ShapeDtypeStruct((t, v_w), BF16),
            jax.ShapeDtypeStruct((t, MEM_WIDTH), BF16),
        ],
        compiler_params=_params(("arbitrary",)),
        name="mla_proj",
    )(h, norm_g.reshape(1, d), positions.reshape(t, 1), freq, w_in_r, q_norm.reshape(1, -1), wq_r,
      kv_norm.reshape(1, -1), wkv_r, kv_all)

    nq = seq // tq
    pairs = [(i, j) for i in range(nq) for j in range(i + 1)]
    qi = jnp.array([p[0] for p in pairs], jnp.int32)
    kj = jnp.array([p[1] for p in pairs], jnp.int32)
    w_out_b = w_out.astype(BF16)
    q_tile = lambda b, s, qi, kj: (b * nq + qi[s], 0)
    k_tile = lambda b, s, qi, kj: (b * nq + kj[s], 0)
    return pl.pallas_call(
        _flash_kernel,
        grid_spec=pltpu.PrefetchScalarGridSpec(
            num_scalar_prefetch=2,
            grid=(batch, len(pairs)),
            in_specs=[
                pl.BlockSpec((tq, qk_w), q_tile),
                pl.BlockSpec((tq, qk_w), k_tile),
                pl.BlockSpec((tq, v_w), k_tile),
                pl.BlockSpec((tq, d), q_tile),
                pl.BlockSpec((tq, MEM_WIDTH), q_tile),
                _resident((v_w, d)),
                _resident((MEM_WIDTH, d)),
            ],
            out_specs=pl.BlockSpec((tq, d), q_tile),
            scratch_shapes=[
                pltpu.VMEM((MLA_HEADS, tq, LANES), F32),
                pltpu.VMEM((MLA_HEADS, tq, 2 * MLA_V), F32),
                pltpu.VMEM((tq, v_w), BF16),
            ],
        ),
        out_shape=jax.ShapeDtypeStruct((t, d), F32),
        compiler_params=_params(("arbitrary", "arbitrary")),
        name="mla_flash",
    )(qi, kj, q, k, v, h, memo, w_out_b[:v_w], w_out_b[v_w:])


def _mlstm_layer_kernel(h_ref, g_ref, win_ref, cw_ref, cb_ref, gb_ref, hg_ref, kv_ref, wout_ref,
                        o_ref, qk_ref, c_ref, m_ref, mix_ref):
    tm = h_ref.shape[0]
    qk_w = 2 * ML_HEADS * ML_DK
    hv = ML_HEADS * ML_DV
    pad = 8

    @pl.when(pl.program_id(1) == 0)
    def _():
        qk_ref[0:pad, :] = jnp.zeros((pad, qk_w), F32)
        c_ref[...] = jnp.zeros_like(c_ref)
        m_ref[...] = jnp.zeros_like(m_ref)

    x = h_ref[...]
    xn = _rms(x, g_ref[...]).astype(BF16)
    proj = _dot(xn, win_ref[...])
    qk_ref[pad:pad + tm, :] = proj[:, :qk_w]
    conv = cb_ref[...]
    for j in range(CONV_K):
        conv = conv + cw_ref[j:j + 1, :] * qk_ref[pad - (CONV_K - 1) + j:pad - (CONV_K - 1) + j + tm, :]
    qk_ref[0:pad, :] = qk_ref[tm:tm + pad, :]
    qk = _silu(conv)
    v_all = proj[:, qk_w:qk_w + hv]
    o_all = proj[:, qk_w + hv:qk_w + 2 * hv]
    mem_q = proj[:, qk_w + 2 * hv:qk_w + 2 * hv + MEM_WIDTH]
    gates = proj[:, qk_w + 2 * hv + MEM_WIDTH:] + gb_ref[...]

    lane = lax.broadcasted_iota(jnp.int32, (CHUNK, LANES), 1)
    row = lax.broadcasted_iota(jnp.int32, (CHUNK, CHUNK), 0)
    col = lax.broadcasted_iota(jnp.int32, (CHUNK, CHUNK), 1)
    causal = col <= row
    tril = _one_hot(causal)
    ones_blk = _one_hot(lane == 0)

    for c in range(tm // CHUNK):
        r0 = c * CHUNK
        gc = gates[r0:r0 + CHUNK]
        logs = jnp.where(lane >= ML_HEADS, jax.nn.log_sigmoid(gc), gc)
        hi = logs.astype(BF16)
        r1 = logs - hi.astype(F32)
        mid = r1.astype(BF16)
        low = (r1 - mid.astype(F32)).astype(BF16)
        cum = _dot(tril, hi) + _dot(tril, mid) + _dot(tril, low)
        col_vals = jnp.where(lane >= ML_HEADS, cum, logs)
        row_vals = col_vals.T
        for hd in range(ML_HEADS):
            q = qk[r0:r0 + CHUNK, hd * ML_DK:(hd + 1) * ML_DK].astype(BF16)
            k32 = qk[r0:r0 + CHUNK, qk_w // 2 + hd * ML_DK:qk_w // 2 + (hd + 1) * ML_DK] * ML_DK ** -0.5
            v = v_all[r0:r0 + CHUNK, hd * ML_DV:(hd + 1) * ML_DV].astype(BF16)
            v_aug = jnp.concatenate([v, ones_blk], axis=1)
            a_col = col_vals[:, ML_HEADS + hd:ML_HEADS + hd + 1]
            li_col = col_vals[:, hd:hd + 1]
            b_row = row_vals[ML_HEADS + hd:ML_HEADS + hd + 1, :]
            li_row = row_vals[hd:hd + 1, :]
            m_in = m_ref[hd]
            c_in = c_ref[hd]

            log_d = jnp.where(causal, a_col - b_row + li_row, -jnp.inf)
            m_intra = jnp.max(log_d, axis=-1, keepdims=True)
            log_inter = a_col + m_in
            m_t = jnp.maximum(log_inter, m_intra)
            inter = jnp.exp(log_inter - m_t)
            p = (_dot_nt(q, k32.astype(BF16)) * jnp.exp(log_d - m_t)).astype(BF16)
            numden = inter * _dot(q, c_in.astype(BF16)) + _dot(p, v_aug)
            den = numden[:, ML_DV:ML_DV + 1]
            h_out = numden[:, :ML_DV] / jnp.maximum(jnp.abs(den), jnp.exp(-m_t))
            h_n = _rms(h_out, hg_ref[:, hd * ML_DV:(hd + 1) * ML_DV])
            gate_o = jax.nn.sigmoid(o_all[r0:r0 + CHUNK, hd * ML_DV:(hd + 1) * ML_DV])
            mix_ref[r0:r0 + CHUNK, hd * ML_DV:(hd + 1) * ML_DV] = (h_n * gate_o).astype(BF16)

            f_tot = b_row[:, CHUNK - 1:CHUNK]
            log_w = f_tot - a_col + li_col
            m_loc = jnp.max(log_w, axis=0, keepdims=True)
            kw_t = (k32 * jnp.exp(log_w - m_loc)).T.astype(BF16)
            c_loc = _dot(kw_t, v_aug)
            m_new = jnp.maximum(f_tot + m_in, m_loc)
            c_ref[hd] = jnp.exp(f_tot + m_in - m_new) * c_in + jnp.exp(m_loc - m_new) * c_loc
            m_ref[hd] = m_new

    mix_ref[:, hv:hv + MEM_WIDTH] = _mem_attention(mem_q, kv_ref).astype(BF16)
    o_ref[...] = x + _dot(mix_ref[...], wout_ref[...])


def _mlstm_layer(h, batch, seq, layer, norm_g, w_in, conv_w, conv_b, gate_b, h_norm, kv_all, w_out,
                 tm=512):
    t, d = h.shape
    qk_w = 2 * ML_HEADS * ML_DK
    hv = ML_HEADS * ML_DV
    n_gate = 2 * ML_HEADS
    lo = qk_w + 2 * hv
    w_in_r = jnp.concatenate(
        [w_in[:, :lo], w_in[:, lo + n_gate:], w_in[:, lo:lo + n_gate],
         jnp.zeros((d, LANES - n_gate), w_in.dtype)], axis=1).astype(BF16)
    gb = jnp.pad(gate_b, (0, LANES - n_gate)).reshape(1, LANES)
    n_in = w_in_r.shape[1]
    n_mix = w_out.shape[0]
    tiles = seq // tm
    m, kvw = kv_all.shape[2], kv_all.shape[3]
    return pl.pallas_call(
        _mlstm_layer_kernel,
        grid=(batch, tiles),
        in_specs=[
            pl.BlockSpec((tm, d), lambda b, i: (b * tiles + i, 0)),
            _resident((1, d)),
            _resident((d, n_in)),
            _resident((CONV_K, qk_w)),
            _resident((1, qk_w)),
            _resident((1, LANES)),
            _resident((1, hv)),
            pl.BlockSpec((1, 1, m, kvw), lambda b, i: (layer, b, 0, 0)),
            _resident((n_mix, d)),
        ],
        out_specs=pl.BlockSpec((tm, d), lambda b, i: (b * tiles + i, 0)),
        out_shape=jax.ShapeDtypeStruct((t, d), F32),
        scratch_shapes=[
            pltpu.VMEM((tm + 8, qk_w), F32),
            pltpu.VMEM((ML_HEADS, ML_DK, ML_AUG), F32),
            pltpu.VMEM((ML_HEADS, 1, 1), F32),
            pltpu.VMEM((tm, n_mix), BF16),
        ],
        compiler_params=_params(("arbitrary", "arbitrary")),
        name="mlstm_layer",
    )(h, norm_g.reshape(1, d), w_in_r, conv_w, conv_b.reshape(1, qk_w), gb, h_norm.reshape(1, hv),
      kv_all, w_out.astype(BF16))


def kernel(x, mem, positions, attn_norm, mem_norm, w_mem_kv, ffn_norm, final_norm, gm_w_in, gm_v_norm, gm_w_s, gm_b_s, gm_w_out, mla_w_in, mla_q_norm, mla_w_uq, mla_kv_norm, mla_w_ukv, mla_w_out, ml_w_in, ml_conv_w, ml_conv_b, ml_gate_b, ml_h_norm, ml_w_out, ff_w_gate, ff_w_up, ff_w_down, moe_w_router, moe_b_router, moe_w_gate, moe_w_up, moe_w_down):
    batch, seq, d = x.shape
    depth = attn_norm.shape[0]
    kv_all = _memkv(mem, mem_norm, w_mem_kv)
    ff_w = [w.astype(BF16) for w in (ff_w_gate, ff_w_up, ff_w_down)]
    moe_w = [w.astype(BF16) for w in (moe_w_gate, moe_w_up, moe_w_down)]
    h = x.reshape(batch * seq, d)
    for layer in range(depth):
        kind, j = layer % 3, layer // 3
        if kind == 0:
            h = _gmlp_layer(h, seq, layer, attn_norm[layer], gm_w_in[j], gm_v_norm[j], gm_w_s[j],
                            gm_b_s[j], kv_all, gm_w_out[j])
        elif kind == 1:
            h = _mla_layer(h, batch, seq, layer, positions, attn_norm[layer], mla_w_in[j],
                           mla_q_norm[j], mla_w_uq[j], mla_kv_norm[j], mla_w_ukv[j], kv_all,
                           mla_w_out[j])
        else:
            h = _mlstm_layer(h, batch, seq, layer, attn_norm[layer], ml_w_in[j], ml_conv_w[j],
                             ml_conv_b[j], ml_gate_b[j], ml_h_norm[j], kv_all, ml_w_out[j])
        c = layer // 2
        last = layer == depth - 1
        if layer % 2 == 0:
            h = _ffn_layer(h, ffn_norm[layer], *ff_w, c)
            if last:
                h = _final_norm(h, final_norm)
        else:
            h = _moe_layer(h, ffn_norm[layer], moe_w_router[c], moe_b_router[c], *moe_w, c,
                           final_norm if last else None)
    return h.reshape(batch, seq, d)


def _final_norm_kernel(h_ref, g_ref, o_ref):
    o_ref[...] = _rms(h_ref[...], g_ref[...])


def _final_norm(h, g, tm=1024):
    t, d = h.shape
    return pl.pallas_call(
        _final_norm_kernel,
        grid=(t // tm,),
        in_specs=[pl.BlockSpec((tm, d), lambda i: (i, 0)), _resident((1, d))],
        out_specs=pl.BlockSpec((tm, d), lambda i: (i, 0)),
        out_shape=jax.ShapeDtypeStruct((t, d), F32),
        compiler_params=_params(("arbitrary",)),
        name="final_norm",
    )(h, g.reshape(1, d))
```

```python
import functools

import jax
import jax.numpy as jnp
from jax import lax
from jax.experimental import pallas as pl
from jax.experimental.pallas import tpu as pltpu

F32 = jnp.float32
BF16 = jnp.bfloat16

NORM_EPS = 1e-6
CHUNK = 128
LANES = 128
MEM_HEADS = 4
MEM_HEAD_DIM = 128
MEM_WIDTH = MEM_HEADS * MEM_HEAD_DIM
GM_GROUPS = 8
GM_GROUP_DIM = 128
GM_WIDTH = GM_GROUPS * GM_GROUP_DIM
MLA_HEADS = 8
MLA_NOPE = 128
MLA_ROPE = 64
MLA_V = 128
MLA_Q_LORA = 768
MLA_KV_LORA = 256
MLA_QK_PAD = 256
ROPE_THETA = 10000.0
ML_HEADS = 4
ML_DV = 256
ML_DK = 128
ML_AUG = ML_DV + LANES
CONV_K = 4
N_EXPERTS = 8
TOP_K = 2

VMEM_LIMIT = 56 * 1024 * 1024

NT_DIMS = (((1,), (1,)), ((), ()))


def _params(semantics, vmem=VMEM_LIMIT):
    return pltpu.CompilerParams(dimension_semantics=semantics, vmem_limit_bytes=vmem)


def _resident(shape, index=None):
    index = (0,) * len(shape) if index is None else index
    return pl.BlockSpec(shape, lambda *_: index, pipeline_mode=pl.Buffered(1))


def _dot(a, b):
    return jnp.dot(a, b, preferred_element_type=F32)


def _dot_nt(a, b):
    return lax.dot_general(a, b, NT_DIMS, preferred_element_type=F32)


def _rms(x, g):
    return x * lax.rsqrt(jnp.mean(x * x, axis=-1, keepdims=True) + NORM_EPS) * g


def _silu(x):
    return x * jax.nn.sigmoid(x)


def _gelu(x):
    return 0.5 * x * (1.0 + lax.erf(x * 0.5 ** 0.5))


def _one_hot(mask):
    return jnp.where(mask, 1.0, 0.0).astype(BF16)


def _memkv_kernel(mem_ref, g_ref, w_ref, kv_ref):
    xn = _rms(mem_ref[0], g_ref[0]).astype(BF16)
    kv_ref[0, 0] = _dot(xn, w_ref[0]).astype(BF16)


def _memkv(mem, mem_norm, w_mem_kv):
    b, m, d = mem.shape
    depth = mem_norm.shape[0]
    n = w_mem_kv.shape[-1]
    return pl.pallas_call(
        _memkv_kernel,
        grid=(depth, b),
        in_specs=[
            pl.BlockSpec((1, m, d), lambda l, i: (i, 0, 0)),
            pl.BlockSpec((1, 1, d), lambda l, i: (l, 0, 0)),
            pl.BlockSpec((1, d, n), lambda l, i: (l, 0, 0)),
        ],
        out_specs=pl.BlockSpec((1, 1, m, n), lambda l, i: (l, i, 0, 0)),
        out_shape=jax.ShapeDtypeStruct((depth, b, m, n), BF16),
        compiler_params=_params(("arbitrary", "arbitrary")),
        name="memkv",
    )(mem, mem_norm.reshape(depth, 1, d), w_mem_kv.astype(BF16))


def _mem_attention(q, kv_ref):
    outs = []
    for hd in range(MEM_HEADS):
        lo = hd * MEM_HEAD_DIM
        qh = q[:, lo:lo + MEM_HEAD_DIM].astype(BF16)
        kh = kv_ref[0, 0, :, lo:lo + MEM_HEAD_DIM]
        vh = kv_ref[0, 0, :, MEM_WIDTH + lo:MEM_WIDTH + lo + MEM_HEAD_DIM]
        s = _dot_nt(qh, kh) * MEM_HEAD_DIM ** -0.5
        p = jnp.exp(s - jnp.max(s, axis=-1, keepdims=True))
        outs.append(_dot(p.astype(BF16), vh) / jnp.sum(p, axis=-1, keepdims=True))
    return jnp.concatenate(outs, axis=-1)


def _gmlp_layer_kernel(h_ref, g_ref, win_ref, vg_ref, ws_ref, bs_ref, kv_ref, wout_ref,
                       o_ref, proj_ref, mix_ref):
    tm = h_ref.shape[0]
    x = h_ref[...]
    xn = _rms(x, g_ref[...]).astype(BF16)
    proj_ref[...] = _dot(xn, win_ref[...])
    row = lax.broadcasted_iota(jnp.int32, (CHUNK, CHUNK), 0)
    col = lax.broadcasted_iota(jnp.int32, (CHUNK, CHUNK), 1)
    causal = col <= row
    for g in range(GM_GROUPS):
        lo = g * GM_GROUP_DIM
        u = _gelu(proj_ref[:, lo:lo + GM_GROUP_DIM])
        v = _gelu(proj_ref[:, GM_WIDTH + lo:GM_WIDTH + lo + GM_GROUP_DIM])
        vn = _rms(v, vg_ref[:, lo:lo + GM_GROUP_DIM]).astype(BF16)
        w = jnp.where(causal, ws_ref[g], 0.0).astype(BF16)
        for c in range(tm // CHUNK):
            r = c * CHUNK
            mixed = _dot(w, vn[r:r + CHUNK]) + bs_ref[g]
            mix_ref[r:r + CHUNK, lo:lo + GM_GROUP_DIM] = (u[r:r + CHUNK] * mixed).astype(BF16)
    mem_o = _mem_attention(proj_ref[:, 2 * GM_WIDTH:2 * GM_WIDTH + MEM_WIDTH], kv_ref)
    mix_ref[:, GM_WIDTH:GM_WIDTH + MEM_WIDTH] = mem_o.astype(BF16)
    o_ref[...] = x + _dot(mix_ref[...], wout_ref[...])


def _gmlp_layer(h, seq, layer, norm_g, w_in, v_norm, w_s, b_s, kv_all, w_out, tm=512):
    t, d = h.shape
    n_in = w_in.shape[1]
    n_mix = w_out.shape[0]
    tiles_per_seq = seq // tm
    m, kvw = kv_all.shape[2], kv_all.shape[3]
    bs_full = jnp.broadcast_to(b_s[:, :, None], (GM_GROUPS, CHUNK, GM_GROUP_DIM)).astype(F32)
    return pl.pallas_call(
        _gmlp_layer_kernel,
        grid=(t // tm,),
        in_specs=[
            pl.BlockSpec((tm, d), lambda i: (i, 0)),
            _resident((1, d)),
            _resident((d, n_in)),
            _resident((1, GM_WIDTH)),
            _resident((GM_GROUPS, CHUNK, CHUNK)),
            _resident((GM_GROUPS, CHUNK, GM_GROUP_DIM)),
            pl.BlockSpec((1, 1, m, kvw), lambda i: (layer, i // tiles_per_seq, 0, 0)),
            _resident((n_mix, d)),
        ],
        out_specs=pl.BlockSpec((tm, d), lambda i: (i, 0)),
        out_shape=jax.ShapeDtypeStruct((t, d), F32),
        scratch_shapes=[pltpu.VMEM((tm, n_in), F32), pltpu.VMEM((tm, n_mix), BF16)],
        compiler_params=_params(("arbitrary",)),
        name="gmlp_layer",
    )(h, norm_g.reshape(1, d), w_in.astype(BF16), v_norm.reshape(1, GM_WIDTH), w_s,
      bs_full, kv_all, w_out.astype(BF16))


def _ffn_kernel(h_ref, g_ref, wg_ref, wu_ref, wd_ref, o_ref, hid_ref, *, fc):
    x = h_ref[...]
    xn = _rms(x, g_ref[...]).astype(BF16)
    f = wg_ref.shape[-1]
    for c in range(f // fc):
        lo = c * fc
        gate = _dot(xn, wg_ref[0, :, lo:lo + fc])
        up = _dot(xn, wu_ref[0, :, lo:lo + fc])
        hid_ref[:, lo:lo + fc] = (_silu(gate) * up).astype(BF16)
    o_ref[...] = x + _dot(hid_ref[...], wd_ref[0])


def _ffn_layer(h, norm_g, w_gate, w_up, w_down, c, tm=512, fc=256):
    t, d = h.shape
    f = w_gate.shape[-1]
    return pl.pallas_call(
        functools.partial(_ffn_kernel, fc=fc),
        grid=(t // tm,),
        in_specs=[
            pl.BlockSpec((tm, d), lambda i: (i, 0)),
            _resident((1, d)),
            _resident((1, d, f), (c, 0, 0)),
            _resident((1, d, f), (c, 0, 0)),
            _resident((1, f, d), (c, 0, 0)),
        ],
        out_specs=pl.BlockSpec((tm, d), lambda i: (i, 0)),
        out_shape=jax.ShapeDtypeStruct((t, d), F32),
        scratch_shapes=[pltpu.VMEM((tm, f), BF16)],
        compiler_params=_params(("arbitrary",)),
        name="ffn_layer",
    )(h, norm_g.reshape(1, d), w_gate, w_up, w_down)


MOE_SLOT_BLOCK = 256
MOE_SLOT_VARIANTS = (256, 288, 320)


def _split_bf16(x):
    hi = x.astype(BF16)
    return hi, (x - hi.astype(F32)).astype(BF16)


def _top2_gates(logits, lane):
    m1 = jnp.max(logits, axis=-1, keepdims=True)
    i1 = jnp.min(jnp.where(logits == m1, lane, LANES), axis=-1, keepdims=True)
    first = lane == i1
    rest = jnp.where(first, -jnp.inf, logits)
    m2 = jnp.max(rest, axis=-1, keepdims=True)
    i2 = jnp.min(jnp.where(rest == m2, lane, LANES), axis=-1, keepdims=True)
    second = lane == i2
    e2 = jnp.exp(m2 - m1)
    denom = 1.0 + e2
    return jnp.where(first, 1.0 / denom, 0.0) + jnp.where(second, e2 / denom, 0.0)


def _moe_kernel(h_ref, g_ref, wr_ref, br_ref, wg_ref, wu_ref, wd_ref, fg_ref,
                o_ref, xn_ref, acc_ref, grow_ref, gcol_ref, rrow_ref, rcol_ref, *, final_norm):
    e = pl.program_id(1)
    tm, d = h_ref.shape

    @pl.when(e == 0)
    def _():
        x = h_ref[...]
        xn = _rms(x, g_ref[...])
        acc_ref[...] = x
        x_hi, x_lo = _split_bf16(xn)
        w_hi, w_lo = _split_bf16(wr_ref[...])
        xn_ref[...] = x_hi
        logits = _dot(x_hi, w_hi) + _dot(x_hi, w_lo) + _dot(x_lo, w_hi) + br_ref[...]
        lane = lax.broadcasted_iota(jnp.int32, (tm, LANES), 1)
        gcol = _top2_gates(jnp.where(lane < N_EXPERTS, logits, -jnp.inf), lane)
        gcol_ref[...] = gcol
        grow_ref[...] = gcol.T[:N_EXPERTS]
        sel = _one_hot(gcol != 0.0)
        r = lax.broadcasted_iota(jnp.int32, (CHUNK, CHUNK), 0)
        c = lax.broadcasted_iota(jnp.int32, (CHUNK, CHUNK), 1)
        earlier = _one_hot(c < r)
        base = jnp.zeros((1, LANES), F32)
        for b in range(tm // CHUNK):
            blk = sel[b * CHUNK:(b + 1) * CHUNK]
            rcol_ref[b * CHUNK:(b + 1) * CHUNK, :] = _dot(earlier, blk) + base
            base = base + jnp.sum(blk.astype(F32), axis=0, keepdims=True)
        rrow_ref[...] = rcol_ref[...].T[:N_EXPERTS]

    gate_row = grow_ref[pl.ds(e, 1), :]
    rank_row = rrow_ref[pl.ds(e, 1), :]
    lane = lax.broadcasted_iota(jnp.int32, (tm, LANES), 1)
    gate_col = jnp.sum(jnp.where(lane == e, gcol_ref[...], 0.0), axis=-1, keepdims=True)
    rank_col = jnp.sum(jnp.where(lane == e, rcol_ref[...], 0.0), axis=-1, keepdims=True)
    count = jnp.sum(jnp.where(gate_row != 0.0, 1.0, 0.0)).astype(jnp.int32)

    def expert_block(cs, first_slot):
        off = jnp.full((1, 1), first_slot, jnp.int32).astype(F32)
        slot_s = lax.broadcasted_iota(jnp.int32, (cs, tm), 0).astype(F32)
        gather = _one_hot(jnp.logical_and(rank_row - off == slot_s, gate_row != 0.0))
        xg = _dot(gather, xn_ref[...]).astype(BF16)
        hid = (_silu(_dot(xg, wg_ref[0, 0])) * _dot(xg, wu_ref[0, 0])).astype(BF16)
        y = _dot(hid, wd_ref[0, 0]).astype(BF16)
        ks = -(-cs // LANES) * LANES
        if ks != cs:
            y = jnp.concatenate([y, jnp.zeros((ks - cs, d), BF16)], axis=0)
        slot_l = lax.broadcasted_iota(jnp.int32, (tm, ks), 1).astype(F32)
        scatter = _one_hot(jnp.logical_and(rank_col - off == slot_l, gate_col != 0.0))
        acc_ref[...] += gate_col * _dot(scatter, y)

    lower = 0
    for cs in MOE_SLOT_VARIANTS:
        @pl.when(jnp.logical_and(count > lower, count <= cs))
        def _(cs=cs):
            expert_block(cs, 0)
        lower = cs

    @pl.when(count > lower)
    def _():
        def body(b, carry):
            expert_block(MOE_SLOT_BLOCK, b * MOE_SLOT_BLOCK)
            return carry
        lax.fori_loop(0, (count + MOE_SLOT_BLOCK - 1) // MOE_SLOT_BLOCK, body, 0)

    @pl.when(e == pl.num_programs(1) - 1)
    def _():
        out = acc_ref[...]
        if final_norm:
            out = _rms(out, fg_ref[...])
        o_ref[...] = out


def _moe_layer(h, norm_g, w_router, b_router, w_gate, w_up, w_down, c, final_g, tm=1024):
    t, d = h.shape
    f = w_gate.shape[-1]
    final_norm = final_g is not None
    fg = (final_g if final_norm else jnp.ones((d,), F32)).reshape(1, d)
    wr = jnp.pad(w_router, ((0, 0), (0, LANES - N_EXPERTS)))
    br = jnp.pad(b_router, (0, LANES - N_EXPERTS)).reshape(1, LANES)
    return pl.pallas_call(
        functools.partial(_moe_kernel, final_norm=final_norm),
        grid=(t // tm, N_EXPERTS),
        in_specs=[
            pl.BlockSpec((tm, d), lambda i, e: (i, 0)),
            _resident((1, d)),
            _resident((d, LANES)),
            _resident((1, LANES)),
            pl.BlockSpec((1, 1, d, f), lambda i, e: (c, e, 0, 0)),
            pl.BlockSpec((1, 1, d, f), lambda i, e: (c, e, 0, 0)),
            pl.BlockSpec((1, 1, f, d), lambda i, e: (c, e, 0, 0)),
            _resident((1, d)),
        ],
        out_specs=pl.BlockSpec((tm, d), lambda i, e: (i, 0)),
        out_shape=jax.ShapeDtypeStruct((t, d), F32),
        scratch_shapes=[
            pltpu.VMEM((tm, d), BF16),
            pltpu.VMEM((tm, d), F32),
            pltpu.VMEM((N_EXPERTS, tm), F32),
            pltpu.VMEM((tm, LANES), F32),
            pltpu.VMEM((N_EXPERTS, tm), F32),
            pltpu.VMEM((tm, LANES), F32),
        ],
        compiler_params=_params(("arbitrary", "arbitrary")),
        name="moe_layer",
    )(h, norm_g.reshape(1, d), wr, br, w_gate, w_up, w_down, fg)


def _mla_proj_kernel(h_ref, g_ref, pos_ref, freq_ref, win_ref, qg_ref, wuq_ref, kvg_ref, wukv_ref,
                     kv_ref, q_out, k_out, v_out, memo_out):
    xn = _rms(h_ref[...], g_ref[...]).astype(BF16)
    proj = _dot(xn, win_ref[...])
    c_q = proj[:, :MLA_Q_LORA]
    c_kv = proj[:, MLA_Q_LORA:MLA_Q_LORA + MLA_KV_LORA]
    lo = MLA_Q_LORA + MLA_KV_LORA
    mem_q = proj[:, lo:lo + MEM_WIDTH]
    k_rope = proj[:, lo + MEM_WIDTH:lo + MEM_WIDTH + LANES]
    memo_out[...] = _mem_attention(mem_q, kv_ref).astype(BF16)

    ang = pos_ref[...].astype(F32) * freq_ref[...]
    lane = lax.broadcasted_iota(jnp.int32, ang.shape, 1)
    cos = jnp.where(lane < MLA_ROPE, jnp.cos(ang), 0.0)
    sin = jnp.sin(ang)
    sin = jnp.where(lane < MLA_ROPE // 2, -sin, jnp.where(lane < MLA_ROPE, sin, 0.0))

    def rope(r):
        return r * cos + pltpu.roll(r, MLA_ROPE // 2, 1) * sin

    scale = (MLA_NOPE + MLA_ROPE) ** -0.5
    q = _dot(_rms(c_q, qg_ref[...]).astype(BF16), wuq_ref[...])
    kvu = _dot(_rms(c_kv, kvg_ref[...]).astype(BF16), wukv_ref[...])
    k_r = rope(k_rope).astype(BF16)
    for hd in range(MLA_HEADS):
        qlo = hd * MLA_QK_PAD
        q_out[:, qlo:qlo + MLA_NOPE] = (q[:, qlo:qlo + MLA_NOPE] * scale).astype(BF16)
        q_out[:, qlo + MLA_NOPE:qlo + MLA_QK_PAD] = (
            rope(q[:, qlo + MLA_NOPE:qlo + MLA_QK_PAD]) * scale).astype(BF16)
        k_out[:, qlo:qlo + MLA_NOPE] = kvu[:, hd * MLA_NOPE:(hd + 1) * MLA_NOPE].astype(BF16)
        k_out[:, qlo + MLA_NOPE:qlo + MLA_QK_PAD] = k_r
    v_out[...] = kvu[:, MLA_HEADS * MLA_NOPE:].astype(BF16)


def _flash_kernel(qi_ref, kj_ref, q_ref, k_ref, v_ref, h_ref, memo_ref, wa_ref, wm_ref, o_ref,
                  m_ref, acc_ref, attn_ref):
    step_id = pl.program_id(1)
    i = qi_ref[step_id]
    j = kj_ref[step_id]
    tq, tk = q_ref.shape[0], k_ref.shape[0]

    @pl.when(j == 0)
    def _():
        m_ref[...] = jnp.full_like(m_ref, -jnp.inf)
        acc_ref[...] = jnp.zeros_like(acc_ref)

    ones = jnp.ones((tk, MLA_V), BF16)

    def step(masked):
        if masked:
            row = lax.broadcasted_iota(jnp.int32, (tq, tk), 0)
            col = lax.broadcasted_iota(jnp.int32, (tq, tk), 1)
            keep = col <= row
        for hd in range(MLA_HEADS):
            q = q_ref[:, hd * MLA_QK_PAD:(hd + 1) * MLA_QK_PAD]
            k = k_ref[:, hd * MLA_QK_PAD:(hd + 1) * MLA_QK_PAD]
            v_aug = jnp.concatenate([v_ref[:, hd * MLA_V:(hd + 1) * MLA_V], ones], axis=1)
            s = _dot_nt(q, k)
            if masked:
                s = jnp.where(keep, s, -jnp.inf)
            m_prev = m_ref[hd]
            m_new = jnp.maximum(m_prev, jnp.max(s, axis=-1, keepdims=True))
            alpha = jnp.exp(m_prev - m_new)
            p = jnp.exp((s - jnp.tile(m_new, (1, tk // LANES))).astype(BF16))
            acc_ref[hd] = jnp.tile(alpha, (1, 2)) * acc_ref[hd] + _dot(p, v_aug)
            m_ref[hd] = m_new

    @pl.when(j < i)
    def _():
        step(False)

    @pl.when(j == i)
    def _():
        step(True)
        for hd in range(MLA_HEADS):
            acc = acc_ref[hd]
            attn_ref[:, hd * MLA_V:(hd + 1) * MLA_V] = (acc[:, :MLA_V] / acc[:, MLA_V:]).astype(BF16)
        o_ref[...] = (h_ref[...] + _dot(attn_ref[...], wa_ref[...])
                      + _dot(memo_ref[...], wm_ref[...]))


def _mla_layer(h, batch, seq, layer, positions, norm_g, w_in, q_norm, w_uq, kv_norm, w_ukv,
               kv_all, w_out, tm=512, tq=512):
    t, d = h.shape
    half = MLA_ROPE // 2
    lo = MLA_Q_LORA + MLA_KV_LORA
    w_kr = w_in[:, lo:lo + MLA_ROPE]
    w_in_r = jnp.concatenate([w_in[:, :lo], w_in[:, lo + MLA_ROPE:], w_kr, w_kr], axis=1).astype(BF16)
    wq = w_uq.reshape(MLA_Q_LORA, MLA_HEADS, MLA_NOPE + MLA_ROPE)
    wq_r = jnp.concatenate([wq, wq[:, :, MLA_NOPE:]], axis=2).reshape(
        MLA_Q_LORA, MLA_HEADS * MLA_QK_PAD).astype(BF16)
    wkv = w_ukv.reshape(MLA_KV_LORA, MLA_HEADS, MLA_NOPE + MLA_V)
    wkv_r = jnp.concatenate([wkv[:, :, :MLA_NOPE].reshape(MLA_KV_LORA, -1),
                             wkv[:, :, MLA_NOPE:].reshape(MLA_KV_LORA, -1)], axis=1).astype(BF16)
    inv_freq = 1.0 / (ROPE_THETA ** (jnp.arange(0, MLA_ROPE, 2, dtype=F32) / MLA_ROPE))
    freq = jnp.tile(inv_freq, LANES // half).reshape(1, LANES)
    n_in = w_in_r.shape[1]
    tiles_per_seq = seq // tm
    m, kvw = kv_all.shape[2], kv_all.shape[3]
    qk_w = MLA_HEADS * MLA_QK_PAD
    v_w = MLA_HEADS * MLA_V
    q, k, v, memo = pl.pallas_call(
        _mla_proj_kernel,
        grid=(t // tm,),
        in_specs=[
            pl.BlockSpec((tm, d), lambda i: (i, 0)),
            _resident((1, d)),
            pl.BlockSpec((tm, 1), lambda i: (i, 0)),
            _resident((1, LANES)),
            _resident((d, n_in)),
            _resident((1, MLA_Q_LORA)),
            _resident((MLA_Q_LORA, qk_w)),
            _resident((1, MLA_KV_LORA)),
            _resident((MLA_KV_LORA, 2 * v_w)),
            pl.BlockSpec((1, 1, m, kvw), lambda i: (layer, i // tiles_per_seq, 0, 0)),
        ],
        out_specs=[
            pl.BlockSpec((tm, qk_w), lambda i: (i, 0)),
            pl.BlockSpec((tm, qk_w), lambda i: (i, 0)),
            pl.BlockSpec((tm, v_w), lambda i: (i, 0)),
            pl.BlockSpec((tm, MEM_WIDTH), lambda i: (i, 0)),
        ],
        out_shape=[
            jax.ShapeDtypeStruct((t, qk_w), BF16),
            jax.ShapeDtypeStruct((t, qk_w), BF16),
            jax.ShapeDtypeStruct((t, v_w), BF16),
            jax.ShapeDtypeStruct((t, MEM_WIDTH), BF16),
        ],
        compiler_params=_params(("arbitrary",)),
        name="mla_proj",
    )(h, norm_g.reshape(1, d), positions.reshape(t, 1), freq, w_in_r, q_norm.reshape(1, -1), wq_r,
      kv_norm.reshape(1, -1), wkv_r, kv_all)

    nq = seq // tq
    pairs = [(i, j) for i in range(nq) for j in range(i + 1)]
    qi = jnp.array([p[0] for p in pairs], jnp.int32)
    kj = jnp.array([p[1] for p in pairs], jnp.int32)
    w_out_b = w_out.astype(BF16)
    q_tile = lambda b, s, qi, kj: (b * nq + qi[s], 0)
    k_tile = lambda b, s, qi, kj: (b * nq + kj[s], 0)
    return pl.pallas_call(
        _flash_kernel,
        grid_spec=pltpu.PrefetchScalarGridSpec(
            num_scalar_prefetch=2,
            grid=(batch, len(pairs)),
            in_specs=[
                pl.BlockSpec((tq, qk_w), q_tile),
                pl.BlockSpec((tq, qk_w), k_tile),
                pl.BlockSpec((tq, v_w), k_tile),
                pl.BlockSpec((tq, d), q_tile),
                pl.BlockSpec((tq, MEM_WIDTH), q_tile),
                _resident((v_w, d)),
                _resident((MEM_WIDTH, d)),
            ],
            out_specs=pl.BlockSpec((tq, d), q_tile),
            scratch_shapes=[
                pltpu.VMEM((MLA_HEADS, tq, LANES), F32),
                pltpu.VMEM((MLA_HEADS, tq, 2 * MLA_V), F32),
                pltpu.VMEM((tq, v_w), BF16),
            ],
        ),
        out_shape=jax.ShapeDtypeStruct((t, d), F32),
        compiler_params=_params(("arbitrary", "arbitrary")),
        name="mla_flash",
    )(qi, kj, q, k, v, h, memo, w_out_b[:v_w], w_out_b[v_w:])


def _mlstm_layer_kernel(h_ref, g_ref, win_ref, cw_ref, cb_ref, gb_ref, hg_ref, kv_ref, wout_ref,
                        o_ref, qk_ref, c_ref, m_ref, mix_ref, *, ck):
    tm = h_ref.shape[0]
    qk_w = 2 * ML_HEADS * ML_DK
    hv = ML_HEADS * ML_DV
    pad = 8

    @pl.when(pl.program_id(1) == 0)
    def _():
        qk_ref[0:pad, :] = jnp.zeros((pad, qk_w), F32)
        c_ref[...] = jnp.zeros_like(c_ref)
        m_ref[...] = jnp.zeros_like(m_ref)

    x = h_ref[...]
    xn = _rms(x, g_ref[...]).astype(BF16)
    proj = _dot(xn, win_ref[...])
    qk_ref[pad:pad + tm, :] = proj[:, :qk_w]
    conv = cb_ref[...]
    for j in range(CONV_K):
        conv = conv + cw_ref[j:j + 1, :] * qk_ref[pad - (CONV_K - 1) + j:pad - (CONV_K - 1) + j + tm, :]
    qk_ref[0:pad, :] = qk_ref[tm:tm + pad, :]
    qk = _silu(conv)
    v_all = proj[:, qk_w:qk_w + hv]
    o_all = proj[:, qk_w + hv:qk_w + 2 * hv]
    mem_q = proj[:, qk_w + 2 * hv:qk_w + 2 * hv + MEM_WIDTH]
    gates = proj[:, qk_w + 2 * hv + MEM_WIDTH:] + gb_ref[...]

    lane = lax.broadcasted_iota(jnp.int32, (ck, LANES), 1)
    row = lax.broadcasted_iota(jnp.int32, (ck, ck), 0)
    col = lax.broadcasted_iota(jnp.int32, (ck, ck), 1)
    causal = col <= row
    tril = _one_hot(causal)
    ones_blk = _one_hot(lane == 0)

    for c in range(tm // ck):
        r0 = c * ck
        gc = gates[r0:r0 + ck]
        logs = jnp.where(lane >= ML_HEADS, jax.nn.log_sigmoid(gc), gc)
        hi = logs.astype(BF16)
        r1 = logs - hi.astype(F32)
        mid = r1.astype(BF16)
        low = (r1 - mid.astype(F32)).astype(BF16)
        cum = _dot(tril, hi) + _dot(tril, mid) + _dot(tril, low)
        col_vals = jnp.where(lane >= ML_HEADS, cum, logs)
        row_vals = col_vals.T
        for hd in range(ML_HEADS):
            q = qk[r0:r0 + ck, hd * ML_DK:(hd + 1) * ML_DK].astype(BF16)
            k32 = qk[r0:r0 + ck, qk_w // 2 + hd * ML_DK:qk_w // 2 + (hd + 1) * ML_DK] * ML_DK ** -0.5
            v = v_all[r0:r0 + ck, hd * ML_DV:(hd + 1) * ML_DV].astype(BF16)
            v_aug = jnp.concatenate([v, ones_blk], axis=1)
            a_col = col_vals[:, ML_HEADS + hd:ML_HEADS + hd + 1]
            li_col = col_vals[:, hd:hd + 1]
            b_row = row_vals[ML_HEADS + hd:ML_HEADS + hd + 1, :]
            li_row = row_vals[hd:hd + 1, :]
            m_in = m_ref[hd]
            c_in = c_ref[hd]

            log_d = jnp.where(causal, a_col - b_row + li_row, -jnp.inf)
            m_intra = jnp.max(log_d, axis=-1, keepdims=True)
            log_inter = a_col + m_in
            m_t = jnp.maximum(log_inter, m_intra)
            inter = jnp.exp(log_inter - m_t)
            p = (_dot_nt(q, k32.astype(BF16)) * jnp.exp(log_d - m_t)).astype(BF16)
            numden = inter * _dot(q, c_in.astype(BF16)) + _dot(p, v_aug)
            den = numden[:, ML_DV:ML_DV + 1]
            h_out = numden[:, :ML_DV] / jnp.maximum(jnp.abs(den), jnp.exp(-m_t))
            h_n = _rms(h_out, hg_ref[:, hd * ML_DV:(hd + 1) * ML_DV])
            gate_o = jax.nn.sigmoid(o_all[r0:r0 + ck, hd * ML_DV:(hd + 1) * ML_DV])
            mix_ref[r0:r0 + ck, hd * ML_DV:(hd + 1) * ML_DV] = (h_n * gate_o).astype(BF16)

            f_tot = b_row[:, ck - 1:ck]
            log_w = f_tot - a_col + li_col
            m_loc = jnp.max(log_w, axis=0, keepdims=True)
            kw_t = (k32 * jnp.exp(log_w - m_loc)).T.astype(BF16)
            c_loc = _dot(kw_t, v_aug)
            m_new = jnp.maximum(f_tot + m_in, m_loc)
            c_ref[hd] = jnp.exp(f_tot + m_in - m_new) * c_in + jnp.exp(m_loc - m_new) * c_loc
            m_ref[hd] = m_new

    mix_ref[:, hv:hv + MEM_WIDTH] = _mem_attention(mem_q, kv_ref).astype(BF16)
    o_ref[...] = x + _dot(mix_ref[...], wout_ref[...])


def _mlstm_layer(h, batch, seq, layer, norm_g, w_in, conv_w, conv_b, gate_b, h_norm, kv_all, w_out,
                 tm=512, ck=256):
    t, d = h.shape
    qk_w = 2 * ML_HEADS * ML_DK
    hv = ML_HEADS * ML_DV
    n_gate = 2 * ML_HEADS
    lo = qk_w + 2 * hv
    w_in_r = jnp.concatenate(
        [w_in[:, :lo], w_in[:, lo + n_gate:], w_in[:, lo:lo + n_gate],
         jnp.zeros((d, LANES - n_gate), w_in.dtype)], axis=1).astype(BF16)
    gb = jnp.pad(gate_b, (0, LANES - n_gate)).reshape(1, LANES)
    n_in = w_in_r.shape[1]
    n_mix = w_out.shape[0]
    tiles = seq // tm
    m, kvw = kv_all.shape[2], kv_all.shape[3]
    return pl.pallas_call(
        functools.partial(_mlstm_layer_kernel, ck=ck),
        grid=(batch, tiles),
        in_specs=[
            pl.BlockSpec((tm, d), lambda b, i: (b * tiles + i, 0)),
            _resident((1, d)),
            _resident((d, n_in)),
            _resident((CONV_K, qk_w)),
            _resident((1, qk_w)),
            _resident((1, LANES)),
            _resident((1, hv)),
            pl.BlockSpec((1, 1, m, kvw), lambda b, i: (layer, b, 0, 0)),
            _resident((n_mix, d)),
        ],
        out_specs=pl.BlockSpec((tm, d), lambda b, i: (b * tiles + i, 0)),
        out_shape=jax.ShapeDtypeStruct((t, d), F32),
        scratch_shapes=[
            pltpu.VMEM((tm + 8, qk_w), F32),
            pltpu.VMEM((ML_HEADS, ML_DK, ML_AUG), F32),
            pltpu.VMEM((ML_HEADS, 1, 1), F32),
            pltpu.VMEM((tm, n_mix), BF16),
        ],
        compiler_params=_params(("arbitrary", "arbitrary")),
        name="mlstm_layer",
    )(h, norm_g.reshape(1, d), w_in_r, conv_w, conv_b.reshape(1, qk_w), gb, h_norm.reshape(1, hv),
      kv_all, w_out.astype(BF16))


def kernel(x, mem, positions, attn_norm, mem_norm, w_mem_kv, ffn_norm, final_norm, gm_w_in, gm_v_norm, gm_w_s, gm_b_s, gm_w_out, mla_w_in, mla_q_norm, mla_w_uq, mla_kv_norm, mla_w_ukv, mla_w_out, ml_w_in, ml_conv_w, ml_conv_b, ml_gate_b, ml_h_norm, ml_w_out, ff_w_gate, ff_w_up, ff_w_down, moe_w_router, moe_b_router, moe_w_gate, moe_w_up, moe_w_down):
    batch, seq, d = x.shape
    depth = attn_norm.shape[0]
    kv_all = _memkv(mem, mem_norm, w_mem_kv)
    ff_w = [w.astype(BF16) for w in (ff_w_gate, ff_w_up, ff_w_down)]
    moe_w = [w.astype(BF16) for w in (moe_w_gate, moe_w_up, moe_w_down)]
    h = x.reshape(batch * seq, d)
    for layer in range(depth):
        kind, j = layer % 3, layer // 3
        if kind == 0:
            h = _gmlp_layer(h, seq, layer, attn_norm[layer], gm_w_in[j], gm_v_norm[j], gm_w_s[j],
                            gm_b_s[j], kv_all, gm_w_out[j])
        elif kind == 1:
            h = _mla_layer(h, batch, seq, layer, positions, attn_norm[layer], mla_w_in[j],
                           mla_q_norm[j], mla_w_uq[j], mla_kv_norm[j], mla_w_ukv[j], kv_all,
                           mla_w_out[j])
        else:
            h = _mlstm_layer(h, batch, seq, layer, attn_norm[layer], ml_w_in[j], ml_conv_w[j],
                             ml_conv_b[j], ml_gate_b[j], ml_h_norm[j], kv_all, ml_w_out[j])
        c = layer // 2
        last = layer == depth - 1
        if layer % 2 == 0:
            h = _ffn_layer(h, ffn_norm[layer], *ff_w, c)
            if last:
                h = _final_norm(h, final_norm)
        else:
            h = _moe_layer(h, ffn_norm[layer], moe_w_router[c], moe_b_router[c], *moe_w, c,
                           final_norm if last else None)
    return h.reshape(batch, seq, d)


def _final_norm_kernel(h_ref, g_ref, o_ref):
    o_ref[...] = _rms(h_ref[...], g_ref[...])


def _final_norm(h, g, tm=1024):
    t, d = h.shape
    return pl.pallas_call(
        _final_norm_kernel,
        grid=(t // tm,),
        in_specs=[pl.BlockSpec((tm, d), lambda i: (i, 0)), _resident((1, d))],
        out_specs=pl.BlockSpec((tm, d), lambda i: (i, 0)),
        out_shape=jax.ShapeDtypeStruct((t, d), F32),
        compiler_params=_params(("arbitrary",)),
        name="final_norm",
    )(h, g.reshape(1, d))
```

```python
import functools

import jax
import jax.numpy as jnp
from jax import lax
from jax.experimental import pallas as pl
from jax.experimental.pallas import tpu as pltpu

F32 = jnp.float32
BF16 = jnp.bfloat16

NORM_EPS = 1e-6
CHUNK = 128
LANES = 128
MEM_HEADS = 4
MEM_HEAD_DIM = 128
MEM_WIDTH = MEM_HEADS * MEM_HEAD_DIM
GM_GROUPS = 8
GM_GROUP_DIM = 128
GM_WIDTH = GM_GROUPS * GM_GROUP_DIM
MLA_HEADS = 8
MLA_NOPE = 128
MLA_ROPE = 64
MLA_V = 128
MLA_Q_LORA = 768
MLA_KV_LORA = 256
MLA_QK_PAD = 256
ROPE_THETA = 10000.0
ML_HEADS = 4
ML_DV = 256
ML_DK = 128
ML_AUG = ML_DV + LANES
CONV_K = 4
N_EXPERTS = 8
TOP_K = 2

VMEM_LIMIT = 56 * 1024 * 1024

NT_DIMS = (((1,), (1,)), ((), ()))


def _params(semantics, vmem=VMEM_LIMIT):
    return pltpu.CompilerParams(dimension_semantics=semantics, vmem_limit_bytes=vmem)


def _resident(shape, index=None):
    index = (0,) * len(shape) if index is None else index
    return pl.BlockSpec(shape, lambda *_: index, pipeline_mode=pl.Buffered(1))


def _dot(a, b):
    return jnp.dot(a, b, preferred_element_type=F32)


def _dot_nt(a, b):
    return lax.dot_general(a, b, NT_DIMS, preferred_element_type=F32)


def _rms(x, g):
    return x * lax.rsqrt(jnp.mean(x * x, axis=-1, keepdims=True) + NORM_EPS) * g


def _silu(x):
    return x * jax.nn.sigmoid(x)


def _gelu(x):
    return 0.5 * x * (1.0 + lax.erf(x * 0.5 ** 0.5))


def _one_hot(mask):
    return jnp.where(mask, 1.0, 0.0).astype(BF16)


def _memkv_kernel(mem_ref, g_ref, w_ref, kv_ref):
    xn = _rms(mem_ref[0], g_ref[0]).astype(BF16)
    kv_ref[0, 0] = _dot(xn, w_ref[0]).astype(BF16)


def _memkv(mem, mem_norm, w_mem_kv):
    b, m, d = mem.shape
    depth = mem_norm.shape[0]
    n = w_mem_kv.shape[-1]
    return pl.pallas_call(
        _memkv_kernel,
        grid=(depth, b),
        in_specs=[
            pl.BlockSpec((1, m, d), lambda l, i: (i, 0, 0)),
            pl.BlockSpec((1, 1, d), lambda l, i: (l, 0, 0)),
            pl.BlockSpec((1, d, n), lambda l, i: (l, 0, 0)),
        ],
        out_specs=pl.BlockSpec((1, 1, m, n), lambda l, i: (l, i, 0, 0)),
        out_shape=jax.ShapeDtypeStruct((depth, b, m, n), BF16),
        compiler_params=_params(("arbitrary", "arbitrary")),
        name="memkv",
    )(mem, mem_norm.reshape(depth, 1, d), w_mem_kv.astype(BF16))


def _mem_attention(q, kv_ref):
    outs = []
    for hd in range(MEM_HEADS):
        lo = hd * MEM_HEAD_DIM
        qh = q[:, lo:lo + MEM_HEAD_DIM].astype(BF16)
        kh = kv_ref[0, 0, :, lo:lo + MEM_HEAD_DIM]
        vh = kv_ref[0, 0, :, MEM_WIDTH + lo:MEM_WIDTH + lo + MEM_HEAD_DIM]
        s = _dot_nt(qh, kh) * MEM_HEAD_DIM ** -0.5
        p = jnp.exp(s - jnp.max(s, axis=-1, keepdims=True))
        outs.append(_dot(p.astype(BF16), vh) / jnp.sum(p, axis=-1, keepdims=True))
    return jnp.concatenate(outs, axis=-1)


FFN_COL_BLOCK = 256


def _ffn_body(x, g_ref, wg_ref, wu_ref, wd_ref, hid_ref):
    xn = _rms(x, g_ref[...]).astype(BF16)
    for lo in range(0, wg_ref.shape[-1], FFN_COL_BLOCK):
        gate = _dot(xn, wg_ref[0, :, lo:lo + FFN_COL_BLOCK])
        up = _dot(xn, wu_ref[0, :, lo:lo + FFN_COL_BLOCK])
        hid_ref[:, lo:lo + FFN_COL_BLOCK] = (_silu(gate) * up).astype(BF16)
    return x + _dot(hid_ref[...], wd_ref[0])


def _ffn_operands(ffn, tm, d):
    norm_g, w_gate, w_up, w_down, c = ffn
    f = w_gate.shape[-1]
    specs = [_resident((1, d)), _resident((1, d, f), (c, 0, 0)), _resident((1, d, f), (c, 0, 0)),
             _resident((1, f, d), (c, 0, 0))]
    return specs, (norm_g.reshape(1, d), w_gate, w_up, w_down), [pltpu.VMEM((tm, f), BF16)]


def _gmlp_layer_kernel(h_ref, g_ref, win_ref, vg_ref, ws_ref, bs_ref, kv_ref, wout_ref, *rest):
    fused_ffn = len(rest) > 3
    if fused_ffn:
        fg_ref, wg_ref, wu_ref, wd_ref, o_ref, proj_ref, mix_ref, hid_ref = rest
    else:
        o_ref, proj_ref, mix_ref = rest
    tm = h_ref.shape[0]
    x = h_ref[...]
    xn = _rms(x, g_ref[...]).astype(BF16)
    proj_ref[...] = _dot(xn, win_ref[...])
    row = lax.broadcasted_iota(jnp.int32, (CHUNK, CHUNK), 0)
    col = lax.broadcasted_iota(jnp.int32, (CHUNK, CHUNK), 1)
    causal = col <= row
    for g in range(GM_GROUPS):
        lo = g * GM_GROUP_DIM
        u = _gelu(proj_ref[:, lo:lo + GM_GROUP_DIM])
        v = _gelu(proj_ref[:, GM_WIDTH + lo:GM_WIDTH + lo + GM_GROUP_DIM])
        vn = _rms(v, vg_ref[:, lo:lo + GM_GROUP_DIM]).astype(BF16)
        w = jnp.where(causal, ws_ref[g], 0.0).astype(BF16)
        for c in range(tm // CHUNK):
            r = c * CHUNK
            mixed = _dot(w, vn[r:r + CHUNK]) + bs_ref[g]
            mix_ref[r:r + CHUNK, lo:lo + GM_GROUP_DIM] = (u[r:r + CHUNK] * mixed).astype(BF16)
    mem_o = _mem_attention(proj_ref[:, 2 * GM_WIDTH:2 * GM_WIDTH + MEM_WIDTH], kv_ref)
    mix_ref[:, GM_WIDTH:GM_WIDTH + MEM_WIDTH] = mem_o.astype(BF16)
    h_mid = x + _dot(mix_ref[...], wout_ref[...])
    if fused_ffn:
        h_mid = _ffn_body(h_mid, fg_ref, wg_ref, wu_ref, wd_ref, hid_ref)
    o_ref[...] = h_mid


def _gmlp_layer(h, seq, layer, norm_g, w_in, v_norm, w_s, b_s, kv_all, w_out, ffn=None, tm=512):
    t, d = h.shape
    n_in = w_in.shape[1]
    n_mix = w_out.shape[0]
    tiles_per_seq = seq // tm
    m, kvw = kv_all.shape[2], kv_all.shape[3]
    bs_full = jnp.broadcast_to(b_s[:, :, None], (GM_GROUPS, CHUNK, GM_GROUP_DIM)).astype(F32)
    ffn_specs, ffn_args, ffn_scratch = _ffn_operands(ffn, tm, d) if ffn else ([], (), [])
    return pl.pallas_call(
        _gmlp_layer_kernel,
        grid=(t // tm,),
        in_specs=[
            pl.BlockSpec((tm, d), lambda i: (i, 0)),
            _resident((1, d)),
            _resident((d, n_in)),
            _resident((1, GM_WIDTH)),
            _resident((GM_GROUPS, CHUNK, CHUNK)),
            _resident((GM_GROUPS, CHUNK, GM_GROUP_DIM)),
            pl.BlockSpec((1, 1, m, kvw), lambda i: (layer, i // tiles_per_seq, 0, 0)),
            _resident((n_mix, d)),
        ] + ffn_specs,
        out_specs=pl.BlockSpec((tm, d), lambda i: (i, 0)),
        out_shape=jax.ShapeDtypeStruct((t, d), F32),
        scratch_shapes=[pltpu.VMEM((tm, n_in), F32), pltpu.VMEM((tm, n_mix), BF16)] + ffn_scratch,
        compiler_params=_params(("arbitrary",)),
        name="gmlp_layer",
    )(h, norm_g.reshape(1, d), w_in.astype(BF16), v_norm.reshape(1, GM_WIDTH), w_s,
      bs_full, kv_all, w_out.astype(BF16), *ffn_args)


def _ffn_kernel(h_ref, g_ref, wg_ref, wu_ref, wd_ref, o_ref, hid_ref):
    o_ref[...] = _ffn_body(h_ref[...], g_ref, wg_ref, wu_ref, wd_ref, hid_ref)


def _ffn_layer(h, ffn, tm=512):
    t, d = h.shape
    ffn_specs, ffn_args, ffn_scratch = _ffn_operands(ffn, tm, d)
    return pl.pallas_call(
        _ffn_kernel,
        grid=(t // tm,),
        in_specs=[pl.BlockSpec((tm, d), lambda i: (i, 0))] + ffn_specs,
        out_specs=pl.BlockSpec((tm, d), lambda i: (i, 0)),
        out_shape=jax.ShapeDtypeStruct((t, d), F32),
        scratch_shapes=ffn_scratch,
        compiler_params=_params(("arbitrary",)),
        name="ffn_layer",
    )(h, *ffn_args)


MOE_SLOT_BLOCK = 256
MOE_SLOT_VARIANTS = (256, 288, 320)


def _split_bf16(x):
    hi = x.astype(BF16)
    return hi, (x - hi.astype(F32)).astype(BF16)


def _top2_gates(logits, lane):
    m1 = jnp.max(logits, axis=-1, keepdims=True)
    i1 = jnp.min(jnp.where(logits == m1, lane, LANES), axis=-1, keepdims=True)
    first = lane == i1
    rest = jnp.where(first, -jnp.inf, logits)
    m2 = jnp.max(rest, axis=-1, keepdims=True)
    i2 = jnp.min(jnp.where(rest == m2, lane, LANES), axis=-1, keepdims=True)
    second = lane == i2
    e2 = jnp.exp(m2 - m1)
    denom = 1.0 + e2
    return jnp.where(first, 1.0 / denom, 0.0) + jnp.where(second, e2 / denom, 0.0)


def _moe_kernel(h_ref, g_ref, wr_ref, br_ref, wg_ref, wu_ref, wd_ref, fg_ref,
                o_ref, xn_ref, grow_ref, gcol_ref, rrow_ref, rcol_ref, *, final_norm):
    e = pl.program_id(1)
    tm, d = h_ref.shape

    @pl.when(e == 0)
    def _():
        x = h_ref[...]
        xn = _rms(x, g_ref[...])
        o_ref[...] = x
        x_hi, x_lo = _split_bf16(xn)
        w_hi, w_lo = _split_bf16(wr_ref[...])
        xn_ref[...] = x_hi
        hi_terms = _dot(x_hi, jnp.concatenate([w_hi, w_lo], axis=1))
        logits = hi_terms[:, :LANES] + hi_terms[:, LANES:] + _dot(x_lo, w_hi) + br_ref[...]
        lane = lax.broadcasted_iota(jnp.int32, (tm, LANES), 1)
        gcol = _top2_gates(jnp.where(lane < N_EXPERTS, logits, -jnp.inf), lane)
        gcol_ref[...] = gcol
        grow_ref[...] = gcol.T[:N_EXPERTS]
        sel = _one_hot(gcol != 0.0)
        r = lax.broadcasted_iota(jnp.int32, (CHUNK, CHUNK), 0)
        c = lax.broadcasted_iota(jnp.int32, (CHUNK, CHUNK), 1)
        earlier = _one_hot(c < r)
        base = jnp.zeros((1, LANES), F32)
        for b in range(tm // CHUNK):
            blk = sel[b * CHUNK:(b + 1) * CHUNK]
            rcol_ref[b * CHUNK:(b + 1) * CHUNK, :] = _dot(earlier, blk) + base
            base = base + jnp.sum(blk.astype(F32), axis=0, keepdims=True)
        rrow_ref[...] = rcol_ref[...].T[:N_EXPERTS]

    gate_row = grow_ref[pl.ds(e, 1), :]
    rank_row = rrow_ref[pl.ds(e, 1), :]
    lane = lax.broadcasted_iota(jnp.int32, (tm, LANES), 1)
    gate_col = jnp.sum(jnp.where(lane == e, gcol_ref[...], 0.0), axis=-1, keepdims=True)
    rank_col = jnp.sum(jnp.where(lane == e, rcol_ref[...], 0.0), axis=-1, keepdims=True)
    count = jnp.sum(jnp.where(gate_row != 0.0, 1.0, 0.0)).astype(jnp.int32)

    def expert_block(cs, first_slot):
        off = jnp.full((1, 1), first_slot, jnp.int32).astype(F32)
        slot_s = lax.broadcasted_iota(jnp.int32, (cs, tm), 0).astype(F32)
        gather = _one_hot(jnp.logical_and(rank_row - off == slot_s, gate_row != 0.0))
        xg = _dot(gather, xn_ref[...]).astype(BF16)
        hid = (_silu(_dot(xg, wg_ref[0, 0])) * _dot(xg, wu_ref[0, 0])).astype(BF16)
        y = _dot(hid, wd_ref[0, 0]).astype(BF16)
        ks = -(-cs // LANES) * LANES
        if ks != cs:
            y = jnp.concatenate([y, jnp.zeros((ks - cs, d), BF16)], axis=0)
        slot_l = lax.broadcasted_iota(jnp.int32, (tm, ks), 1).astype(F32)
        scatter = _one_hot(jnp.logical_and(rank_col - off == slot_l, gate_col != 0.0))
        o_ref[...] += gate_col * _dot(scatter, y)

    lower = 0
    for cs in MOE_SLOT_VARIANTS:
        @pl.when(jnp.logical_and(count > lower, count <= cs))
        def _(cs=cs):
            expert_block(cs, 0)
        lower = cs

    @pl.when(count > lower)
    def _():
        def body(b, carry):
            expert_block(MOE_SLOT_BLOCK, b * MOE_SLOT_BLOCK)
            return carry
        lax.fori_loop(0, (count + MOE_SLOT_BLOCK - 1) // MOE_SLOT_BLOCK, body, 0)

    if final_norm:
        @pl.when(e == pl.num_programs(1) - 1)
        def _():
            o_ref[...] = _rms(o_ref[...], fg_ref[...])


def _moe_layer(h, norm_g, w_router, b_router, w_gate, w_up, w_down, c, final_g, tm=1024):
    t, d = h.shape
    f = w_gate.shape[-1]
    final_norm = final_g is not None
    fg = (final_g if final_norm else jnp.ones((d,), F32)).reshape(1, d)
    wr = jnp.pad(w_router, ((0, 0), (0, LANES - N_EXPERTS)))
    br = jnp.pad(b_router, (0, LANES - N_EXPERTS)).reshape(1, LANES)
    return pl.pallas_call(
        functools.partial(_moe_kernel, final_norm=final_norm),
        grid=(t // tm, N_EXPERTS),
        in_specs=[
            pl.BlockSpec((tm, d), lambda i, e: (i, 0)),
            _resident((1, d)),
            _resident((d, LANES)),
            _resident((1, LANES)),
            pl.BlockSpec((1, 1, d, f), lambda i, e: (c, e, 0, 0)),
            pl.BlockSpec((1, 1, d, f), lambda i, e: (c, e, 0, 0)),
            pl.BlockSpec((1, 1, f, d), lambda i, e: (c, e, 0, 0)),
            _resident((1, d)),
        ],
        out_specs=pl.BlockSpec((tm, d), lambda i, e: (i, 0)),
        out_shape=jax.ShapeDtypeStruct((t, d), F32),
        scratch_shapes=[
            pltpu.VMEM((tm, d), BF16),
            pltpu.VMEM((N_EXPERTS, tm), F32),
            pltpu.VMEM((tm, LANES), F32),
            pltpu.VMEM((N_EXPERTS, tm), F32),
            pltpu.VMEM((tm, LANES), F32),
        ],
        compiler_params=_params(("arbitrary", "arbitrary")),
        name="moe_layer",
    )(h, norm_g.reshape(1, d), wr, br, w_gate, w_up, w_down, fg)


def _mla_proj_kernel(h_ref, g_ref, pos_ref, freq_ref, win_ref, qg_ref, wuq_ref, kvg_ref, wukv_ref,
                     kv_ref, q_out, k_out, v_out, memo_out):
    xn = _rms(h_ref[...], g_ref[...]).astype(BF16)
    proj = _dot(xn, win_ref[...])
    c_q = proj[:, :MLA_Q_LORA]
    c_kv = proj[:, MLA_Q_LORA:MLA_Q_LORA + MLA_KV_LORA]
    lo = MLA_Q_LORA + MLA_KV_LORA
    mem_q = proj[:, lo:lo + MEM_WIDTH]
    k_rope = proj[:, lo + MEM_WIDTH:lo + MEM_WIDTH + LANES]
    memo_out[...] = _mem_attention(mem_q, kv_ref).astype(BF16)

    ang = pos_ref[...].astype(F32) * freq_ref[...]
    lane = lax.broadcasted_iota(jnp.int32, ang.shape, 1)
    cos = jnp.where(lane < MLA_ROPE, jnp.cos(ang), 0.0)
    sin = jnp.sin(ang)
    sin = jnp.where(lane < MLA_ROPE // 2, -sin, jnp.where(lane < MLA_ROPE, sin, 0.0))

    def rope(r):
        return r * cos + pltpu.roll(r, MLA_ROPE // 2, 1) * sin

    scale = (MLA_NOPE + MLA_ROPE) ** -0.5
    q = _dot(_rms(c_q, qg_ref[...]).astype(BF16), wuq_ref[...])
    kvu = _dot(_rms(c_kv, kvg_ref[...]).astype(BF16), wukv_ref[...])
    k_r = rope(k_rope).astype(BF16)
    for hd in range(MLA_HEADS):
        qlo = hd * MLA_QK_PAD
        q_out[:, qlo:qlo + MLA_NOPE] = (q[:, qlo:qlo + MLA_NOPE] * scale).astype(BF16)
        q_out[:, qlo + MLA_NOPE:qlo + MLA_QK_PAD] = (
            rope(q[:, qlo + MLA_NOPE:qlo + MLA_QK_PAD]) * scale).astype(BF16)
        k_out[:, qlo:qlo + MLA_NOPE] = kvu[:, hd * MLA_NOPE:(hd + 1) * MLA_NOPE].astype(BF16)
        k_out[:, qlo + MLA_NOPE:qlo + MLA_QK_PAD] = k_r
    v_out[...] = kvu[:, MLA_HEADS * MLA_NOPE:].astype(BF16)


def _flash_kernel(qi_ref, kj_ref, q_ref, k_ref, v_ref, h_ref, memo_ref, wa_ref, wm_ref, o_ref,
                  m_ref, acc_ref, attn_ref):
    step_id = pl.program_id(1)
    i = qi_ref[step_id]
    j = kj_ref[step_id]
    tq, tk = q_ref.shape[0], k_ref.shape[0]

    @pl.when(j == 0)
    def _():
        m_ref[...] = jnp.full_like(m_ref, -jnp.inf)
        acc_ref[...] = jnp.zeros_like(acc_ref)

    ones = jnp.ones((tk, MLA_V), BF16)

    def step(masked):
        if masked:
            row = lax.broadcasted_iota(jnp.int32, (tq, tk), 0)
            col = lax.broadcasted_iota(jnp.int32, (tq, tk), 1)
            keep = col <= row
        for hd in range(MLA_HEADS):
            q = q_ref[:, hd * MLA_QK_PAD:(hd + 1) * MLA_QK_PAD]
            k = k_ref[:, hd * MLA_QK_PAD:(hd + 1) * MLA_QK_PAD]
            v_aug = jnp.concatenate([v_ref[:, hd * MLA_V:(hd + 1) * MLA_V], ones], axis=1)
            s = _dot_nt(q, k)
            if masked:
                s = jnp.where(keep, s, -jnp.inf)
            m_prev = m_ref[hd]
            m_new = jnp.maximum(m_prev, jnp.max(s, axis=-1, keepdims=True))
            alpha = jnp.exp(m_prev - m_new)
            p = jnp.exp((s - jnp.tile(m_new, (1, tk // LANES))).astype(BF16))
            acc_ref[hd] = jnp.tile(alpha, (1, 2)) * acc_ref[hd] + _dot(p, v_aug)
            m_ref[hd] = m_new

    @pl.when(j < i)
    def _():
        step(False)

    @pl.when(j == i)
    def _():
        step(True)
        for hd in range(MLA_HEADS):
            acc = acc_ref[hd]
            attn_ref[:, hd * MLA_V:(hd + 1) * MLA_V] = (acc[:, :MLA_V] / acc[:, MLA_V:]).astype(BF16)
        o_ref[...] = (h_ref[...] + _dot(attn_ref[...], wa_ref[...])
                      + _dot(memo_ref[...], wm_ref[...]))


def _mla_layer(h, batch, seq, layer, positions, norm_g, w_in, q_norm, w_uq, kv_norm, w_ukv,
               kv_all, w_out, tm=512, tq=512):
    t, d = h.shape
    half = MLA_ROPE // 2
    lo = MLA_Q_LORA + MLA_KV_LORA
    w_kr = w_in[:, lo:lo + MLA_ROPE]
    w_in_r = jnp.concatenate([w_in[:, :lo], w_in[:, lo + MLA_ROPE:], w_kr, w_kr], axis=1).astype(BF16)
    wq = w_uq.reshape(MLA_Q_LORA, MLA_HEADS, MLA_NOPE + MLA_ROPE)
    wq_r = jnp.concatenate([wq, wq[:, :, MLA_NOPE:]], axis=2).reshape(
        MLA_Q_LORA, MLA_HEADS * MLA_QK_PAD).astype(BF16)
    wkv = w_ukv.reshape(MLA_KV_LORA, MLA_HEADS, MLA_NOPE + MLA_V)
    wkv_r = jnp.concatenate([wkv[:, :, :MLA_NOPE].reshape(MLA_KV_LORA, -1),
                             wkv[:, :, MLA_NOPE:].reshape(MLA_KV_LORA, -1)], axis=1).astype(BF16)
    inv_freq = 1.0 / (ROPE_THETA ** (jnp.arange(0, MLA_ROPE, 2, dtype=F32) / MLA_ROPE))
    freq = jnp.tile(inv_freq, LANES // half).reshape(1, LANES)
    n_in = w_in_r.shape[1]
    tiles_per_seq = seq // tm
    m, kvw = kv_all.shape[2], kv_all.shape[3]
    qk_w = MLA_HEADS * MLA_QK_PAD
    v_w = MLA_HEADS * MLA_V
    q, k, v, memo = pl.pallas_call(
        _mla_proj_kernel,
        grid=(t // tm,),
        in_specs=[
            pl.BlockSpec((tm, d), lambda i: (i, 0)),
            _resident((1, d)),
            pl.BlockSpec((tm, 1), lambda i: (i, 0)),
            _resident((1, LANES)),
            _resident((d, n_in)),
            _resident((1, MLA_Q_LORA)),
            _resident((MLA_Q_LORA, qk_w)),
            _resident((1, MLA_KV_LORA)),
            _resident((MLA_KV_LORA, 2 * v_w)),
            pl.BlockSpec((1, 1, m, kvw), lambda i: (layer, i // tiles_per_seq, 0, 0)),
        ],
        out_specs=[
            pl.BlockSpec((tm, qk_w), lambda i: (i, 0)),
            pl.BlockSpec((tm, qk_w), lambda i: (i, 0)),
            pl.BlockSpec((tm, v_w), lambda i: (i, 0)),
            pl.BlockSpec((tm, MEM_WIDTH), lambda i: (i, 0)),
        ],
        out_shape=[
            jax.ShapeDtypeStruct((t, qk_w), BF16),
            jax.ShapeDtypeStruct((t, qk_w), BF16),
            jax.ShapeDtypeStruct((t, v_w), BF16),
            jax.ShapeDtypeStruct((t, MEM_WIDTH), BF16),
        ],
        compiler_params=_params(("arbitrary",)),
        name="mla_proj",
    )(h, norm_g.reshape(1, d), positions.reshape(t, 1), freq, w_in_r, q_norm.reshape(1, -1), wq_r,
      kv_norm.reshape(1, -1), wkv_r, kv_all)

    nq = seq // tq
    pairs = [(i, j) for i in range(nq) for j in range(i + 1)]
    qi = jnp.array([p[0] for p in pairs], jnp.int32)
    kj = jnp.array([p[1] for p in pairs], jnp.int32)
    w_out_b = w_out.astype(BF16)
    q_tile = lambda b, s, qi, kj: (b * nq + qi[s], 0)
    k_tile = lambda b, s, qi, kj: (b * nq + kj[s], 0)
    return pl.pallas_call(
        _flash_kernel,
        grid_spec=pltpu.PrefetchScalarGridSpec(
            num_scalar_prefetch=2,
            grid=(batch, len(pairs)),
            in_specs=[
                pl.BlockSpec((tq, qk_w), q_tile),
                pl.BlockSpec((tq, qk_w), k_tile),
                pl.BlockSpec((tq, v_w), k_tile),
                pl.BlockSpec((tq, d), q_tile),
                pl.BlockSpec((tq, MEM_WIDTH), q_tile),
                _resident((v_w, d)),
                _resident((MEM_WIDTH, d)),
            ],
            out_specs=pl.BlockSpec((tq, d), q_tile),
            scratch_shapes=[
                pltpu.VMEM((MLA_HEADS, tq, LANES), F32),
                pltpu.VMEM((MLA_HEADS, tq, 2 * MLA_V), F32),
                pltpu.VMEM((tq, v_w), BF16),
            ],
        ),
        out_shape=jax.ShapeDtypeStruct((t, d), F32),
        compiler_params=_params(("arbitrary", "arbitrary")),
        name="mla_flash",
    )(qi, kj, q, k, v, h, memo, w_out_b[:v_w], w_out_b[v_w:])


def _mlstm_layer_kernel(h_ref, g_ref, win_ref, cw_ref, cb_ref, gb_ref, hg_ref, kv_ref, wout_ref,
                        *rest, ck):
    fused_ffn = len(rest) > 5
    if fused_ffn:
        fg_ref, wg_ref, wu_ref, wd_ref, o_ref, qk_ref, c_ref, m_ref, mix_ref, hid_ref = rest
    else:
        o_ref, qk_ref, c_ref, m_ref, mix_ref = rest
    tm = h_ref.shape[0]
    qk_w = 2 * ML_HEADS * ML_DK
    hv = ML_HEADS * ML_DV
    pad = 8

    @pl.when(pl.program_id(1) == 0)
    def _():
        qk_ref[0:pad, :] = jnp.zeros((pad, qk_w), F32)
        c_ref[...] = jnp.zeros_like(c_ref)
        m_ref[...] = jnp.zeros_like(m_ref)

    x = h_ref[...]
    xn = _rms(x, g_ref[...]).astype(BF16)
    proj = _dot(xn, win_ref[...])
    qk_ref[pad:pad + tm, :] = proj[:, :qk_w]
    conv = cb_ref[...]
    for j in range(CONV_K):
        conv = conv + cw_ref[j:j + 1, :] * qk_ref[pad - (CONV_K - 1) + j:pad - (CONV_K - 1) + j + tm, :]
    qk_ref[0:pad, :] = qk_ref[tm:tm + pad, :]
    qk = _silu(conv)
    v_all = proj[:, qk_w:qk_w + hv]
    o_all = proj[:, qk_w + hv:qk_w + 2 * hv]
    mem_q = proj[:, qk_w + 2 * hv:qk_w + 2 * hv + MEM_WIDTH]
    gates = proj[:, qk_w + 2 * hv + MEM_WIDTH:] + gb_ref[...]

    lane = lax.broadcasted_iota(jnp.int32, (ck, LANES), 1)
    row = lax.broadcasted_iota(jnp.int32, (ck, ck), 0)
    col = lax.broadcasted_iota(jnp.int32, (ck, ck), 1)
    causal = col <= row
    tril = _one_hot(causal)
    ones_blk = _one_hot(lane == 0)

    for c in range(tm // ck):
        r0 = c * ck
        gc = gates[r0:r0 + ck]
        logs = jnp.where(lane >= ML_HEADS, jax.nn.log_sigmoid(gc), gc)
        hi = logs.astype(BF16)
        r1 = logs - hi.astype(F32)
        mid = r1.astype(BF16)
        low = (r1 - mid.astype(F32)).astype(BF16)
        cum = _dot(tril, hi) + _dot(tril, mid) + _dot(tril, low)
        col_vals = jnp.where(lane >= ML_HEADS, cum, logs)
        row_vals = col_vals.T
        for hd in range(ML_HEADS):
            q = qk[r0:r0 + ck, hd * ML_DK:(hd + 1) * ML_DK].astype(BF16)
            k32 = qk[r0:r0 + ck, qk_w // 2 + hd * ML_DK:qk_w // 2 + (hd + 1) * ML_DK] * ML_DK ** -0.5
            v = v_all[r0:r0 + ck, hd * ML_DV:(hd + 1) * ML_DV].astype(BF16)
            v_aug = jnp.concatenate([v, ones_blk], axis=1)
            a_col = col_vals[:, ML_HEADS + hd:ML_HEADS + hd + 1]
            li_col = col_vals[:, hd:hd + 1]
            b_row = row_vals[ML_HEADS + hd:ML_HEADS + hd + 1, :]
            li_row = row_vals[hd:hd + 1, :]
            m_in = m_ref[hd]
            c_in = c_ref[hd]

            log_d = jnp.where(causal, a_col - b_row + li_row, -jnp.inf)
            m_intra = jnp.max(log_d, axis=-1, keepdims=True)
            log_inter = a_col + m_in
            m_t = jnp.maximum(log_inter, m_intra)
            inter = jnp.exp(log_inter - m_t)
            p = (_dot_nt(q, k32.astype(BF16)) * jnp.exp(log_d - m_t)).astype(BF16)
            numden = inter * _dot(q, c_in.astype(BF16)) + _dot(p, v_aug)
            den = numden[:, ML_DV:ML_DV + 1]
            h_out = numden[:, :ML_DV] / jnp.maximum(jnp.abs(den), jnp.exp(-m_t))
            h_n = _rms(h_out, hg_ref[:, hd * ML_DV:(hd + 1) * ML_DV])
            gate_o = jax.nn.sigmoid(o_all[r0:r0 + ck, hd * ML_DV:(hd + 1) * ML_DV])
            mix_ref[r0:r0 + ck, hd * ML_DV:(hd + 1) * ML_DV] = (h_n * gate_o).astype(BF16)

            f_tot = b_row[:, ck - 1:ck]
            log_w = f_tot - a_col + li_col
            m_loc = jnp.max(log_w, axis=0, keepdims=True)
            kw_t = (k32 * jnp.exp(log_w - m_loc)).T.astype(BF16)
            c_loc = _dot(kw_t, v_aug)
            m_new = jnp.maximum(f_tot + m_in, m_loc)
            c_ref[hd] = jnp.exp(f_tot + m_in - m_new) * c_in + jnp.exp(m_loc - m_new) * c_loc
            m_ref[hd] = m_new

    mix_ref[:, hv:hv + MEM_WIDTH] = _mem_attention(mem_q, kv_ref).astype(BF16)
    h_mid = x + _dot(mix_ref[...], wout_ref[...])
    if fused_ffn:
        h_mid = _ffn_body(h_mid, fg_ref, wg_ref, wu_ref, wd_ref, hid_ref)
    o_ref[...] = h_mid


def _mlstm_layer(h, batch, seq, layer, norm_g, w_in, conv_w, conv_b, gate_b, h_norm, kv_all, w_out,
                 ffn=None, tm=512, ck=256):
    t, d = h.shape
    ffn_specs, ffn_args, ffn_scratch = _ffn_operands(ffn, tm, d) if ffn else ([], (), [])
    qk_w = 2 * ML_HEADS * ML_DK
    hv = ML_HEADS * ML_DV
    n_gate = 2 * ML_HEADS
    lo = qk_w + 2 * hv
    w_in_r = jnp.concatenate(
        [w_in[:, :lo], w_in[:, lo + n_gate:], w_in[:, lo:lo + n_gate],
         jnp.zeros((d, LANES - n_gate), w_in.dtype)], axis=1).astype(BF16)
    gb = jnp.pad(gate_b, (0, LANES - n_gate)).reshape(1, LANES)
    n_in = w_in_r.shape[1]
    n_mix = w_out.shape[0]
    tiles = seq // tm
    m, kvw = kv_all.shape[2], kv_all.shape[3]
    return pl.pallas_call(
        functools.partial(_mlstm_layer_kernel, ck=ck),
        grid=(batch, tiles),
        in_specs=[
            pl.BlockSpec((tm, d), lambda b, i: (b * tiles + i, 0)),
            _resident((1, d)),
            _resident((d, n_in)),
            _resident((CONV_K, qk_w)),
            _resident((1, qk_w)),
            _resident((1, LANES)),
            _resident((1, hv)),
            pl.BlockSpec((1, 1, m, kvw), lambda b, i: (layer, b, 0, 0)),
            _resident((n_mix, d)),
        ] + ffn_specs,
        out_specs=pl.BlockSpec((tm, d), lambda b, i: (b * tiles + i, 0)),
        out_shape=jax.ShapeDtypeStruct((t, d), F32),
        scratch_shapes=[
            pltpu.VMEM((tm + 8, qk_w), F32),
            pltpu.VMEM((ML_HEADS, ML_DK, ML_AUG), F32),
            pltpu.VMEM((ML_HEADS, 1, 1), F32),
            pltpu.VMEM((tm, n_mix), BF16),
        ] + ffn_scratch,
        compiler_params=_params(("arbitrary", "arbitrary")),
        name="mlstm_layer",
    )(h, norm_g.reshape(1, d), w_in_r, conv_w, conv_b.reshape(1, qk_w), gb, h_norm.reshape(1, hv),
      kv_all, w_out.astype(BF16), *ffn_args)


def kernel(x, mem, positions, attn_norm, mem_norm, w_mem_kv, ffn_norm, final_norm, gm_w_in, gm_v_norm, gm_w_s, gm_b_s, gm_w_out, mla_w_in, mla_q_norm, mla_w_uq, mla_kv_norm, mla_w_ukv, mla_w_out, ml_w_in, ml_conv_w, ml_conv_b, ml_gate_b, ml_h_norm, ml_w_out, ff_w_gate, ff_w_up, ff_w_down, moe_w_router, moe_b_router, moe_w_gate, moe_w_up, moe_w_down):
    batch, seq, d = x.shape
    depth = attn_norm.shape[0]
    kv_all = _memkv(mem, mem_norm, w_mem_kv)
    ff_w = [w.astype(BF16) for w in (ff_w_gate, ff_w_up, ff_w_down)]
    moe_w = [w.astype(BF16) for w in (moe_w_gate, moe_w_up, moe_w_down)]
    h = x.reshape(batch * seq, d)
    for layer in range(depth):
        kind, j = layer % 3, layer // 3
        c = layer // 2
        last = layer == depth - 1
        ffn = (ffn_norm[layer], *ff_w, c) if layer % 2 == 0 else None
        if kind == 0:
            h = _gmlp_layer(h, seq, layer, attn_norm[layer], gm_w_in[j], gm_v_norm[j], gm_w_s[j],
                            gm_b_s[j], kv_all, gm_w_out[j], ffn)
        elif kind == 1:
            h = _mla_layer(h, batch, seq, layer, positions, attn_norm[layer], mla_w_in[j],
                           mla_q_norm[j], mla_w_uq[j], mla_kv_norm[j], mla_w_ukv[j], kv_all,
                           mla_w_out[j])
            if ffn:
                h = _ffn_layer(h, ffn)
        else:
            h = _mlstm_layer(h, batch, seq, layer, attn_norm[layer], ml_w_in[j], ml_conv_w[j],
                             ml_conv_b[j], ml_gate_b[j], ml_h_norm[j], kv_all, ml_w_out[j], ffn)
        if layer % 2 == 0:
            if last:
                h = _final_norm(h, final_norm)
        else:
            h = _moe_layer(h, ffn_norm[layer], moe_w_router[c], moe_b_router[c], *moe_w, c,
                           final_norm if last else None)
    return h.reshape(batch, seq, d)


def _final_norm_kernel(h_ref, g_ref, o_ref):
    o_ref[...] = _rms(h_ref[...], g_ref[...])


def _final_norm(h, g, tm=1024):
    t, d = h.shape
    return pl.pallas_call(
        _final_norm_kernel,
        grid=(t // tm,),
        in_specs=[pl.BlockSpec((tm, d), lambda i: (i, 0)), _resident((1, d))],
        out_specs=pl.BlockSpec((tm, d), lambda i: (i, 0)),
        out_shape=jax.ShapeDtypeStruct((t, d), F32),
        compiler_params=_params(("arbitrary",)),
        name="final_norm",
    )(h, g.reshape(1, d))
```

```python
import functools

import jax
import jax.numpy as jnp
from jax import lax
from jax.experimental import pallas as pl
from jax.experimental.pallas import tpu as pltpu

F32 = jnp.float32
BF16 = jnp.bfloat16

NORM_EPS = 1e-6
CHUNK = 128
LANES = 128
MEM_HEADS = 4
MEM_HEAD_DIM = 128
MEM_WIDTH = MEM_HEADS * MEM_HEAD_DIM
GM_GROUPS = 8
GM_GROUP_DIM = 128
GM_WIDTH = GM_GROUPS * GM_GROUP_DIM
MLA_HEADS = 8
MLA_NOPE = 128
MLA_ROPE = 64
MLA_V = 128
MLA_Q_LORA = 768
MLA_KV_LORA = 256
MLA_QK_PAD = 256
ROPE_THETA = 10000.0
ML_HEADS = 4
ML_DV = 256
ML_DK = 128
ML_AUG = ML_DV + LANES
CONV_K = 4
N_EXPERTS = 8
TOP_K = 2

VMEM_LIMIT = 56 * 1024 * 1024

NT_DIMS = (((1,), (1,)), ((), ()))


def _params(semantics, vmem=VMEM_LIMIT):
    return pltpu.CompilerParams(dimension_semantics=semantics, vmem_limit_bytes=vmem)


def _resident(shape, index=None):
    index = (0,) * len(shape) if index is None else index
    return pl.BlockSpec(shape, lambda *_: index, pipeline_mode=pl.Buffered(1))


def _dot(a, b):
    return jnp.dot(a, b, preferred_element_type=F32)


def _dot_nt(a, b):
    return lax.dot_general(a, b, NT_DIMS, preferred_element_type=F32)


def _rms(x, g):
    return x * lax.rsqrt(jnp.mean(x * x, axis=-1, keepdims=True) + NORM_EPS) * g


def _silu(x):
    return x * jax.nn.sigmoid(x)


def _gelu(x):
    return 0.5 * x * (1.0 + lax.erf(x * 0.5 ** 0.5))


def _one_hot(mask):
    return jnp.where(mask, 1.0, 0.0).astype(BF16)


def _memkv_kernel(mem_ref, g_ref, w_ref, kv_ref):
    xn = _rms(mem_ref[0], g_ref[0]).astype(BF16)
    kv_ref[0, 0] = _dot(xn, w_ref[0]).astype(BF16)


def _memkv(mem, mem_norm, w_mem_kv):
    b, m, d = mem.shape
    depth = mem_norm.shape[0]
    n = w_mem_kv.shape[-1]
    return pl.pallas_call(
        _memkv_kernel,
        grid=(depth, b),
        in_specs=[
            pl.BlockSpec((1, m, d), lambda l, i: (i, 0, 0)),
            pl.BlockSpec((1, 1, d), lambda l, i: (l, 0, 0)),
            pl.BlockSpec((1, d, n), lambda l, i: (l, 0, 0)),
        ],
        out_specs=pl.BlockSpec((1, 1, m, n), lambda l, i: (l, i, 0, 0)),
        out_shape=jax.ShapeDtypeStruct((depth, b, m, n), BF16),
        compiler_params=_params(("arbitrary", "arbitrary")),
        name="memkv",
    )(mem, mem_norm.reshape(depth, 1, d), w_mem_kv.astype(BF16))


def _mem_attention(q, kv_ref):
    outs = []
    for hd in range(MEM_HEADS):
        lo = hd * MEM_HEAD_DIM
        qh = q[:, lo:lo + MEM_HEAD_DIM].astype(BF16)
        kh = kv_ref[0, 0, :, lo:lo + MEM_HEAD_DIM]
        vh = kv_ref[0, 0, :, MEM_WIDTH + lo:MEM_WIDTH + lo + MEM_HEAD_DIM]
        s = _dot_nt(qh, kh) * MEM_HEAD_DIM ** -0.5
        p = jnp.exp(s - jnp.max(s, axis=-1, keepdims=True))
        outs.append(_dot(p.astype(BF16), vh) / jnp.sum(p, axis=-1, keepdims=True))
    return jnp.concatenate(outs, axis=-1)


def _gmlp_layer_kernel(h_ref, g_ref, win_ref, vg_ref, ws_ref, bs_ref, kv_ref, wout_ref,
                       o_ref, proj_ref, mix_ref):
    tm = h_ref.shape[0]
    x = h_ref[...]
    xn = _rms(x, g_ref[...]).astype(BF16)
    proj_ref[...] = _dot(xn, win_ref[...])
    row = lax.broadcasted_iota(jnp.int32, (CHUNK, CHUNK), 0)
    col = lax.broadcasted_iota(jnp.int32, (CHUNK, CHUNK), 1)
    causal = col <= row
    for g in range(GM_GROUPS):
        lo = g * GM_GROUP_DIM
        u = _gelu(proj_ref[:, lo:lo + GM_GROUP_DIM])
        v = _gelu(proj_ref[:, GM_WIDTH + lo:GM_WIDTH + lo + GM_GROUP_DIM])
        vn = _rms(v, vg_ref[:, lo:lo + GM_GROUP_DIM]).astype(BF16)
        w = jnp.where(causal, ws_ref[g], 0.0).astype(BF16)
        for c in range(tm // CHUNK):
            r = c * CHUNK
            mixed = _dot(w, vn[r:r + CHUNK]) + bs_ref[g]
            mix_ref[r:r + CHUNK, lo:lo + GM_GROUP_DIM] = (u[r:r + CHUNK] * mixed).astype(BF16)
    mem_o = _mem_attention(proj_ref[:, 2 * GM_WIDTH:2 * GM_WIDTH + MEM_WIDTH], kv_ref)
    mix_ref[:, GM_WIDTH:GM_WIDTH + MEM_WIDTH] = mem_o.astype(BF16)
    o_ref[...] = x + _dot(mix_ref[...], wout_ref[...])


def _gmlp_layer(h, seq, layer, norm_g, w_in, v_norm, w_s, b_s, kv_all, w_out, tm=512):
    t, d = h.shape
    n_in = w_in.shape[1]
    n_mix = w_out.shape[0]
    tiles_per_seq = seq // tm
    m, kvw = kv_all.shape[2], kv_all.shape[3]
    bs_full = jnp.broadcast_to(b_s[:, :, None], (GM_GROUPS, CHUNK, GM_GROUP_DIM)).astype(F32)
    return pl.pallas_call(
        _gmlp_layer_kernel,
        grid=(t // tm,),
        in_specs=[
            pl.BlockSpec((tm, d), lambda i: (i, 0)),
            _resident((1, d)),
            _resident((d, n_in)),
            _resident((1, GM_WIDTH)),
            _resident((GM_GROUPS, CHUNK, CHUNK)),
            _resident((GM_GROUPS, CHUNK, GM_GROUP_DIM)),
            pl.BlockSpec((1, 1, m, kvw), lambda i: (layer, i // tiles_per_seq, 0, 0)),
            _resident((n_mix, d)),
        ],
        out_specs=pl.BlockSpec((tm, d), lambda i: (i, 0)),
        out_shape=jax.ShapeDtypeStruct((t, d), F32),
        scratch_shapes=[pltpu.VMEM((tm, n_in), F32), pltpu.VMEM((tm, n_mix), BF16)],
        compiler_params=_params(("arbitrary",)),
        name="gmlp_layer",
    )(h, norm_g.reshape(1, d), w_in.astype(BF16), v_norm.reshape(1, GM_WIDTH), w_s,
      bs_full, kv_all, w_out.astype(BF16))


FFN_COL_BLOCK = 256


def _ffn_kernel(h_ref, g_ref, wg_ref, wu_ref, wd_ref, o_ref, hid_ref):
    x = h_ref[...]
    xn = _rms(x, g_ref[...]).astype(BF16)
    for lo in range(0, wg_ref.shape[-1], FFN_COL_BLOCK):
        gate = _dot(xn, wg_ref[0, :, lo:lo + FFN_COL_BLOCK])
        up = _dot(xn, wu_ref[0, :, lo:lo + FFN_COL_BLOCK])
        hid_ref[:, lo:lo + FFN_COL_BLOCK] = (_silu(gate) * up).astype(BF16)
    o_ref[...] = x + _dot(hid_ref[...], wd_ref[0])


def _ffn_layer(h, norm_g, w_gate, w_up, w_down, c, tm=512):
    t, d = h.shape
    f = w_gate.shape[-1]
    return pl.pallas_call(
        _ffn_kernel,
        grid=(t // tm,),
        in_specs=[
            pl.BlockSpec((tm, d), lambda i: (i, 0)),
            _resident((1, d)),
            _resident((1, d, f), (c, 0, 0)),
            _resident((1, d, f), (c, 0, 0)),
            _resident((1, f, d), (c, 0, 0)),
        ],
        out_specs=pl.BlockSpec((tm, d), lambda i: (i, 0)),
        out_shape=jax.ShapeDtypeStruct((t, d), F32),
        scratch_shapes=[pltpu.VMEM((tm, f), BF16)],
        compiler_params=_params(("arbitrary",)),
        name="ffn_layer",
    )(h, norm_g.reshape(1, d), w_gate, w_up, w_down)


MOE_SLOT_BLOCK = 256
MOE_SLOT_VARIANTS = (256, 288, 320)


def _split_bf16(x):
    hi = x.astype(BF16)
    return hi, (x - hi.astype(F32)).astype(BF16)


def _top2_gates(logits, lane):
    m1 = jnp.max(logits, axis=-1, keepdims=True)
    i1 = jnp.min(jnp.where(logits == m1, lane, LANES), axis=-1, keepdims=True)
    first = lane == i1
    rest = jnp.where(first, -jnp.inf, logits)
    m2 = jnp.max(rest, axis=-1, keepdims=True)
    i2 = jnp.min(jnp.where(rest == m2, lane, LANES), axis=-1, keepdims=True)
    second = lane == i2
    e2 = jnp.exp(m2 - m1)
    denom = 1.0 + e2
    return jnp.where(first, 1.0 / denom, 0.0) + jnp.where(second, e2 / denom, 0.0)


def _moe_kernel(h_ref, g_ref, wr_ref, br_ref, wg_ref, wu_ref, wd_ref, fg_ref,
                o_ref, xn_ref, grow_ref, gcol_ref, rrow_ref, rcol_ref, *, final_norm):
    e = pl.program_id(1)
    tm, d = h_ref.shape

    @pl.when(e == 0)
    def _():
        x = h_ref[...]
        xn = _rms(x, g_ref[...])
        o_ref[...] = x
        x_hi, x_lo = _split_bf16(xn)
        w_hi, w_lo = _split_bf16(wr_ref[...])
        xn_ref[...] = x_hi
        hi_terms = _dot(x_hi, jnp.concatenate([w_hi, w_lo], axis=1))
        logits = hi_terms[:, :LANES] + hi_terms[:, LANES:] + _dot(x_lo, w_hi) + br_ref[...]
        lane = lax.broadcasted_iota(jnp.int32, (tm, LANES), 1)
        gcol = _top2_gates(jnp.where(lane < N_EXPERTS, logits, -jnp.inf), lane)
        gcol_ref[...] = gcol
        grow_ref[...] = gcol.T[:N_EXPERTS]
        sel = _one_hot(gcol != 0.0)
        r = lax.broadcasted_iota(jnp.int32, (CHUNK, CHUNK), 0)
        c = lax.broadcasted_iota(jnp.int32, (CHUNK, CHUNK), 1)
        earlier = _one_hot(c < r)
        base = jnp.zeros((1, LANES), F32)
        for b in range(tm // CHUNK):
            blk = sel[b * CHUNK:(b + 1) * CHUNK]
            rcol_ref[b * CHUNK:(b + 1) * CHUNK, :] = _dot(earlier, blk) + base
            base = base + jnp.sum(blk.astype(F32), axis=0, keepdims=True)
        rrow_ref[...] = rcol_ref[...].T[:N_EXPERTS]

    gate_row = grow_ref[pl.ds(e, 1), :]
    rank_row = rrow_ref[pl.ds(e, 1), :]
    lane = lax.broadcasted_iota(jnp.int32, (tm, LANES), 1)
    gate_col = jnp.sum(jnp.where(lane == e, gcol_ref[...], 0.0), axis=-1, keepdims=True)
    rank_col = jnp.sum(jnp.where(lane == e, rcol_ref[...], 0.0), axis=-1, keepdims=True)
    count = jnp.sum(jnp.where(gate_row != 0.0, 1.0, 0.0)).astype(jnp.int32)

    def expert_block(cs, first_slot):
        off = jnp.full((1, 1), first_slot, jnp.int32).astype(F32)
        slot_s = lax.broadcasted_iota(jnp.int32, (cs, tm), 0).astype(F32)
        gather = _one_hot(jnp.logical_and(rank_row - off == slot_s, gate_row != 0.0))
        xg = _dot(gather, xn_ref[...]).astype(BF16)
        hid = (_silu(_dot(xg, wg_ref[0, 0])) * _dot(xg, wu_ref[0, 0])).astype(BF16)
        y = _dot(hid, wd_ref[0, 0]).astype(BF16)
        ks = -(-cs // LANES) * LANES
        if ks != cs:
            y = jnp.concatenate([y, jnp.zeros((ks - cs, d), BF16)], axis=0)
        slot_l = lax.broadcasted_iota(jnp.int32, (tm, ks), 1).astype(F32)
        scatter = _one_hot(jnp.logical_and(rank_col - off == slot_l, gate_col != 0.0))
        o_ref[...] += gate_col * _dot(scatter, y)

    lower = 0
    for cs in MOE_SLOT_VARIANTS:
        @pl.when(jnp.logical_and(count > lower, count <= cs))
        def _(cs=cs):
            expert_block(cs, 0)
        lower = cs

    @pl.when(count > lower)
    def _():
        def body(b, carry):
            expert_block(MOE_SLOT_BLOCK, b * MOE_SLOT_BLOCK)
            return carry
        lax.fori_loop(0, (count + MOE_SLOT_BLOCK - 1) // MOE_SLOT_BLOCK, body, 0)

    if final_norm:
        @pl.when(e == pl.num_programs(1) - 1)
        def _():
            o_ref[...] = _rms(o_ref[...], fg_ref[...])


def _moe_layer(h, norm_g, w_router, b_router, w_gate, w_up, w_down, c, final_g, tm=1024):
    t, d = h.shape
    f = w_gate.shape[-1]
    final_norm = final_g is not None
    fg = (final_g if final_norm else jnp.ones((d,), F32)).reshape(1, d)
    wr = jnp.pad(w_router, ((0, 0), (0, LANES - N_EXPERTS)))
    br = jnp.pad(b_router, (0, LANES - N_EXPERTS)).reshape(1, LANES)
    return pl.pallas_call(
        functools.partial(_moe_kernel, final_norm=final_norm),
        grid=(t // tm, N_EXPERTS),
        in_specs=[
            pl.BlockSpec((tm, d), lambda i, e: (i, 0)),
            _resident((1, d)),
            _resident((d, LANES)),
            _resident((1, LANES)),
            pl.BlockSpec((1, 1, d, f), lambda i, e: (c, e, 0, 0)),
            pl.BlockSpec((1, 1, d, f), lambda i, e: (c, e, 0, 0)),
            pl.BlockSpec((1, 1, f, d), lambda i, e: (c, e, 0, 0)),
            _resident((1, d)),
        ],
        out_specs=pl.BlockSpec((tm, d), lambda i, e: (i, 0)),
        out_shape=jax.ShapeDtypeStruct((t, d), F32),
        scratch_shapes=[
            pltpu.VMEM((tm, d), BF16),
            pltpu.VMEM((N_EXPERTS, tm), F32),
            pltpu.VMEM((tm, LANES), F32),
            pltpu.VMEM((N_EXPERTS, tm), F32),
            pltpu.VMEM((tm, LANES), F32),
        ],
        compiler_params=_params(("arbitrary", "arbitrary")),
        name="moe_layer",
    )(h, norm_g.reshape(1, d), wr, br, w_gate, w_up, w_down, fg)


def _mla_proj_kernel(h_ref, g_ref, pos_ref, freq_ref, win_ref, qg_ref, wuq_ref, kvg_ref, wukv_ref,
                     kv_ref, q_out, k_out, v_out, memo_out):
    xn = _rms(h_ref[...], g_ref[...]).astype(BF16)
    proj = _dot(xn, win_ref[...])
    c_q = proj[:, :MLA_Q_LORA]
    c_kv = proj[:, MLA_Q_LORA:MLA_Q_LORA + MLA_KV_LORA]
    lo = MLA_Q_LORA + MLA_KV_LORA
    mem_q = proj[:, lo:lo + MEM_WIDTH]
    k_rope = proj[:, lo + MEM_WIDTH:lo + MEM_WIDTH + LANES]
    memo_out[...] = _mem_attention(mem_q, kv_ref).astype(BF16)

    ang = pos_ref[...].astype(F32) * freq_ref[...]
    lane = lax.broadcasted_iota(jnp.int32, ang.shape, 1)
    cos = jnp.where(lane < MLA_ROPE, jnp.cos(ang), 0.0)
    sin = jnp.sin(ang)
    sin = jnp.where(lane < MLA_ROPE // 2, -sin, jnp.where(lane < MLA_ROPE, sin, 0.0))

    def rope(r):
        return r * cos + pltpu.roll(r, MLA_ROPE // 2, 1) * sin

    scale = (MLA_NOPE + MLA_ROPE) ** -0.5
    q = _dot(_rms(c_q, qg_ref[...]).astype(BF16), wuq_ref[...])
    kvu = _dot(_rms(c_kv, kvg_ref[...]).astype(BF16), wukv_ref[...])
    k_r = rope(k_rope).astype(BF16)
    for hd in range(MLA_HEADS):
        qlo = hd * MLA_QK_PAD
        q_out[:, qlo:qlo + MLA_NOPE] = (q[:, qlo:qlo + MLA_NOPE] * scale).astype(BF16)
        q_out[:, qlo + MLA_NOPE:qlo + MLA_QK_PAD] = (
            rope(q[:, qlo + MLA_NOPE:qlo + MLA_QK_PAD]) * scale).astype(BF16)
        k_out[:, qlo:qlo + MLA_NOPE] = kvu[:, hd * MLA_NOPE:(hd + 1) * MLA_NOPE].astype(BF16)
        k_out[:, qlo + MLA_NOPE:qlo + MLA_QK_PAD] = k_r
    v_out[...] = kvu[:, MLA_HEADS * MLA_NOPE:].astype(BF16)


def _flash_kernel(qi_ref, kj_ref, q_ref, k_ref, v_ref, h_ref, memo_ref, wa_ref, wm_ref, o_ref,
                  m_ref, acc_ref, attn_ref):
    step_id = pl.program_id(1)
    i = qi_ref[step_id]
    j = kj_ref[step_id]
    tq, tk = q_ref.shape[0], k_ref.shape[0]

    @pl.when(j == 0)
    def _():
        m_ref[...] = jnp.full_like(m_ref, -jnp.inf)
        acc_ref[...] = jnp.zeros_like(acc_ref)

    ones = jnp.ones((tk, MLA_V), BF16)

    def step(masked):
        if masked:
            row = lax.broadcasted_iota(jnp.int32, (tq, tk), 0)
            col = lax.broadcasted_iota(jnp.int32, (tq, tk), 1)
            keep = col <= row
        for hd in range(MLA_HEADS):
            q = q_ref[:, hd * MLA_QK_PAD:(hd + 1) * MLA_QK_PAD]
            k = k_ref[:, hd * MLA_QK_PAD:(hd + 1) * MLA_QK_PAD]
            v_aug = jnp.concatenate([v_ref[:, hd * MLA_V:(hd + 1) * MLA_V], ones], axis=1)
            s = _dot_nt(q, k)
            if masked:
                s = jnp.where(keep, s, -jnp.inf)
            m_prev = m_ref[hd]
            m_new = jnp.maximum(m_prev, jnp.max(s, axis=-1, keepdims=True))
            alpha = jnp.exp(m_prev - m_new)
            p = jnp.exp((s - jnp.tile(m_new, (1, tk // LANES))).astype(BF16))
            acc_ref[hd] = jnp.tile(alpha, (1, 2)) * acc_ref[hd] + _dot(p, v_aug)
            m_ref[hd] = m_new

    @pl.when(j < i)
    def _():
        step(False)

    @pl.when(j == i)
    def _():
        step(True)
        for hd in range(MLA_HEADS):
            acc = acc_ref[hd]
            attn_ref[:, hd * MLA_V:(hd + 1) * MLA_V] = (acc[:, :MLA_V] / acc[:, MLA_V:]).astype(BF16)
        o_ref[...] = (h_ref[...] + _dot(attn_ref[...], wa_ref[...])
                      + _dot(memo_ref[...], wm_ref[...]))


def _mla_layer(h, batch, seq, layer, positions, norm_g, w_in, q_norm, w_uq, kv_norm, w_ukv,
               kv_all, w_out, tm=512, tq=512):
    t, d = h.shape
    half = MLA_ROPE // 2
    lo = MLA_Q_LORA + MLA_KV_LORA
    w_kr = w_in[:, lo:lo + MLA_ROPE]
    w_in_r = jnp.concatenate([w_in[:, :lo], w_in[:, lo + MLA_ROPE:], w_kr, w_kr], axis=1).astype(BF16)
    wq = w_uq.reshape(MLA_Q_LORA, MLA_HEADS, MLA_NOPE + MLA_ROPE)
    wq_r = jnp.concatenate([wq, wq[:, :, MLA_NOPE:]], axis=2).reshape(
        MLA_Q_LORA, MLA_HEADS * MLA_QK_PAD).astype(BF16)
    wkv = w_ukv.reshape(MLA_KV_LORA, MLA_HEADS, MLA_NOPE + MLA_V)
    wkv_r = jnp.concatenate([wkv[:, :, :MLA_NOPE].reshape(MLA_KV_LORA, -1),
                             wkv[:, :, MLA_NOPE:].reshape(MLA_KV_LORA, -1)], axis=1).astype(BF16)
    inv_freq = 1.0 / (ROPE_THETA ** (jnp.arange(0, MLA_ROPE, 2, dtype=F32) / MLA_ROPE))
    freq = jnp.tile(inv_freq, LANES // half).reshape(1, LANES)
    n_in = w_in_r.shape[1]
    tiles_per_seq = seq // tm
    m, kvw = kv_all.shape[2], kv_all.shape[3]
    qk_w = MLA_HEADS * MLA_QK_PAD
    v_w = MLA_HEADS * MLA_V
    q, k, v, memo = pl.pallas_call(
        _mla_proj_kernel,
        grid=(t // tm,),
        in_specs=[
            pl.BlockSpec((tm, d), lambda i: (i, 0)),
            _resident((1, d)),
            pl.BlockSpec((tm, 1), lambda i: (i, 0)),
            _resident((1, LANES)),
            _resident((d, n_in)),
            _resident((1, MLA_Q_LORA)),
            _resident((MLA_Q_LORA, qk_w)),
            _resident((1, MLA_KV_LORA)),
            _resident((MLA_KV_LORA, 2 * v_w)),
            pl.BlockSpec((1, 1, m, kvw), lambda i: (layer, i // tiles_per_seq, 0, 0)),
        ],
        out_specs=[
            pl.BlockSpec((tm, qk_w), lambda i: (i, 0)),
            pl.BlockSpec((tm, qk_w), lambda i: (i, 0)),
            pl.BlockSpec((tm, v_w), lambda i: (i, 0)),
            pl.BlockSpec((tm, MEM_WIDTH), lambda i: (i, 0)),
        ],
        out_shape=[
            jax.ShapeDtypeStruct((t, qk_w), BF16),
            jax.ShapeDtypeStruct((t, qk_w), BF16),
            jax.ShapeDtypeStruct((t, v_w), BF16),
            jax.ShapeDtypeStruct((t, MEM_WIDTH), BF16),
        ],
        compiler_params=_params(("arbitrary",)),
        name="mla_proj",
    )(h, norm_g.reshape(1, d), positions.reshape(t, 1), freq, w_in_r, q_norm.reshape(1, -1), wq_r,
      kv_norm.reshape(1, -1), wkv_r, kv_all)

    nq = seq // tq
    pairs = [(i, j) for i in range(nq) for j in range(i + 1)]
    qi = jnp.array([p[0] for p in pairs], jnp.int32)
    kj = jnp.array([p[1] for p in pairs], jnp.int32)
    w_out_b = w_out.astype(BF16)
    q_tile = lambda b, s, qi, kj: (b * nq + qi[s], 0)
    k_tile = lambda b, s, qi, kj: (b * nq + kj[s], 0)
    return pl.pallas_call(
        _flash_kernel,
        grid_spec=pltpu.PrefetchScalarGridSpec(
            num_scalar_prefetch=2,
            grid=(batch, len(pairs)),
            in_specs=[
                pl.BlockSpec((tq, qk_w), q_tile),
                pl.BlockSpec((tq, qk_w), k_tile),
                pl.BlockSpec((tq, v_w), k_tile),
                pl.BlockSpec((tq, d), q_tile),
                pl.BlockSpec((tq, MEM_WIDTH), q_tile),
                _resident((v_w, d)),
                _resident((MEM_WIDTH, d)),
            ],
            out_specs=pl.BlockSpec((tq, d), q_tile),
            scratch_shapes=[
                pltpu.VMEM((MLA_HEADS, tq, LANES), F32),
                pltpu.VMEM((MLA_HEADS, tq, 2 * MLA_V), F32),
                pltpu.VMEM((tq, v_w), BF16),
            ],
        ),
        out_shape=jax.ShapeDtypeStruct((t, d), F32),
        compiler_params=_params(("arbitrary", "arbitrary")),
        name="mla_flash",
    )(qi, kj, q, k, v, h, memo, w_out_b[:v_w], w_out_b[v_w:])


def _mlstm_layer_kernel(h_ref, g_ref, win_ref, cw_ref, cb_ref, gb_ref, hg_ref, kv_ref, wout_ref,
                        o_ref, qk_ref, c_ref, m_ref, mix_ref, *, ck):
    tm = h_ref.shape[0]
    qk_w = 2 * ML_HEADS * ML_DK
    hv = ML_HEADS * ML_DV
    pad = 8

    @pl.when(pl.program_id(1) == 0)
    def _():
        qk_ref[0:pad, :] = jnp.zeros((pad, qk_w), F32)
        c_ref[...] = jnp.zeros_like(c_ref)
        m_ref[...] = jnp.zeros_like(m_ref)

    x = h_ref[...]
    xn = _rms(x, g_ref[...]).astype(BF16)
    proj = _dot(xn, win_ref[...])
    qk_ref[pad:pad + tm, :] = proj[:, :qk_w]
    conv = cb_ref[...]
    for j in range(CONV_K):
        conv = conv + cw_ref[j:j + 1, :] * qk_ref[pad - (CONV_K - 1) + j:pad - (CONV_K - 1) + j + tm, :]
    qk_ref[0:pad, :] = qk_ref[tm:tm + pad, :]
    qk = _silu(conv)
    v_all = proj[:, qk_w:qk_w + hv]
    o_all = proj[:, qk_w + hv:qk_w + 2 * hv]
    mem_q = proj[:, qk_w + 2 * hv:qk_w + 2 * hv + MEM_WIDTH]
    gates = proj[:, qk_w + 2 * hv + MEM_WIDTH:] + gb_ref[...]

    lane = lax.broadcasted_iota(jnp.int32, (ck, LANES), 1)
    row = lax.broadcasted_iota(jnp.int32, (ck, ck), 0)
    col = lax.broadcasted_iota(jnp.int32, (ck, ck), 1)
    causal = col <= row
    tril = _one_hot(causal)
    ones_blk = _one_hot(lane == 0)

    for c in range(tm // ck):
        r0 = c * ck
        gc = gates[r0:r0 + ck]
        logs = jnp.where(lane >= ML_HEADS, jax.nn.log_sigmoid(gc), gc)
        hi = logs.astype(BF16)
        r1 = logs - hi.astype(F32)
        mid = r1.astype(BF16)
        low = (r1 - mid.astype(F32)).astype(BF16)
        cum = _dot(tril, hi) + _dot(tril, mid) + _dot(tril, low)
        col_vals = jnp.where(lane >= ML_HEADS, cum, logs)
        row_vals = col_vals.T
        for hd in range(ML_HEADS):
            q = qk[r0:r0 + ck, hd * ML_DK:(hd + 1) * ML_DK].astype(BF16)
            k32 = qk[r0:r0 + ck, qk_w // 2 + hd * ML_DK:qk_w // 2 + (hd + 1) * ML_DK] * ML_DK ** -0.5
            v = v_all[r0:r0 + ck, hd * ML_DV:(hd + 1) * ML_DV].astype(BF16)
            v_aug = jnp.concatenate([v, ones_blk], axis=1)
            a_col = col_vals[:, ML_HEADS + hd:ML_HEADS + hd + 1]
            li_col = col_vals[:, hd:hd + 1]
            b_row = row_vals[ML_HEADS + hd:ML_HEADS + hd + 1, :]
            li_row = row_vals[hd:hd + 1, :]
            m_in = m_ref[hd]
            c_in = c_ref[hd]

            log_d = jnp.where(causal, a_col - b_row + li_row, -jnp.inf)
            m_intra = jnp.max(log_d, axis=-1, keepdims=True)
            log_inter = a_col + m_in
            m_t = jnp.maximum(log_inter, m_intra)
            inter = jnp.exp(log_inter - m_t)
            p = (_dot_nt(q, k32.astype(BF16)) * jnp.exp(log_d - m_t)).astype(BF16)
            numden = inter * _dot(q, c_in.astype(BF16)) + _dot(p, v_aug)
            den = numden[:, ML_DV:ML_DV + 1]
            h_out = numden[:, :ML_DV] / jnp.maximum(jnp.abs(den), jnp.exp(-m_t))
            h_n = _rms(h_out, hg_ref[:, hd * ML_DV:(hd + 1) * ML_DV])
            gate_o = jax.nn.sigmoid(o_all[r0:r0 + ck, hd * ML_DV:(hd + 1) * ML_DV])
            mix_ref[r0:r0 + ck, hd * ML_DV:(hd + 1) * ML_DV] = (h_n * gate_o).astype(BF16)

            f_tot = b_row[:, ck - 1:ck]
            log_w = f_tot - a_col + li_col
            m_loc = jnp.max(log_w, axis=0, keepdims=True)
            kw_t = (k32 * jnp.exp(log_w - m_loc)).T.astype(BF16)
            c_loc = _dot(kw_t, v_aug)
            m_new = jnp.maximum(f_tot + m_in, m_loc)
            c_ref[hd] = jnp.exp(f_tot + m_in - m_new) * c_in + jnp.exp(m_loc - m_new) * c_loc
            m_ref[hd] = m_new

    mix_ref[:, hv:hv + MEM_WIDTH] = _mem_attention(mem_q, kv_ref).astype(BF16)
    o_ref[...] = x + _dot(mix_ref[...], wout_ref[...])


def _mlstm_layer(h, batch, seq, layer, norm_g, w_in, conv_w, conv_b, gate_b, h_norm, kv_all, w_out,
                 tm=512, ck=256):
    t, d = h.shape
    qk_w = 2 * ML_HEADS * ML_DK
    hv = ML_HEADS * ML_DV
    n_gate = 2 * ML_HEADS
    lo = qk_w + 2 * hv
    w_in_r = jnp.concatenate(
        [w_in[:, :lo], w_in[:, lo + n_gate:], w_in[:, lo:lo + n_gate],
         jnp.zeros((d, LANES - n_gate), w_in.dtype)], axis=1).astype(BF16)
    gb = jnp.pad(gate_b, (0, LANES - n_gate)).reshape(1, LANES)
    n_in = w_in_r.shape[1]
    n_mix = w_out.shape[0]
    tiles = seq // tm
    m, kvw = kv_all.shape[2], kv_all.shape[3]
    return pl.pallas_call(
        functools.partial(_mlstm_layer_kernel, ck=ck),
        grid=(batch, tiles),
        in_specs=[
            pl.BlockSpec((tm, d), lambda b, i: (b * tiles + i, 0)),
            _resident((1, d)),
            _resident((d, n_in)),
            _resident((CONV_K, qk_w)),
            _resident((1, qk_w)),
            _resident((1, LANES)),
            _resident((1, hv)),
            pl.BlockSpec((1, 1, m, kvw), lambda b, i: (layer, b, 0, 0)),
            _resident((n_mix, d)),
        ],
        out_specs=pl.BlockSpec((tm, d), lambda b, i: (b * tiles + i, 0)),
        out_shape=jax.ShapeDtypeStruct((t, d), F32),
        scratch_shapes=[
            pltpu.VMEM((tm + 8, qk_w), F32),
            pltpu.VMEM((ML_HEADS, ML_DK, ML_AUG), F32),
            pltpu.VMEM((ML_HEADS, 1, 1), F32),
            pltpu.VMEM((tm, n_mix), BF16),
        ],
        compiler_params=_params(("arbitrary", "arbitrary")),
        name="mlstm_layer",
    )(h, norm_g.reshape(1, d), w_in_r, conv_w, conv_b.reshape(1, qk_w), gb, h_norm.reshape(1, hv),
      kv_all, w_out.astype(BF16))


def kernel(x, mem, positions, attn_norm, mem_norm, w_mem_kv, ffn_norm, final_norm, gm_w_in, gm_v_norm, gm_w_s, gm_b_s, gm_w_out, mla_w_in, mla_q_norm, mla_w_uq, mla_kv_norm, mla_w_ukv, mla_w_out, ml_w_in, ml_conv_w, ml_conv_b, ml_gate_b, ml_h_norm, ml_w_out, ff_w_gate, ff_w_up, ff_w_down, moe_w_router, moe_b_router, moe_w_gate, moe_w_up, moe_w_down):
    batch, seq, d = x.shape
    depth = attn_norm.shape[0]
    kv_all = _memkv(mem, mem_norm, w_mem_kv)
    ff_w = [w.astype(BF16) for w in (ff_w_gate, ff_w_up, ff_w_down)]
    moe_w = [w.astype(BF16) for w in (moe_w_gate, moe_w_up, moe_w_down)]
    h = x.reshape(batch * seq, d)
    for layer in range(depth):
        kind, j = layer % 3, layer // 3
        c = layer // 2
        last = layer == depth - 1
        if kind == 0:
            h = _gmlp_layer(h, seq, layer, attn_norm[layer], gm_w_in[j], gm_v_norm[j], gm_w_s[j],
                            gm_b_s[j], kv_all, gm_w_out[j])
        elif kind == 1:
            h = _mla_layer(h, batch, seq, layer, positions, attn_norm[layer], mla_w_in[j],
                           mla_q_norm[j], mla_w_uq[j], mla_kv_norm[j], mla_w_ukv[j], kv_all,
                           mla_w_out[j])
        else:
            h = _mlstm_layer(h, batch, seq, layer, attn_norm[layer], ml_w_in[j], ml_conv_w[j],
                             ml_conv_b[j], ml_gate_b[j], ml_h_norm[j], kv_all, ml_w_out[j])
        if layer % 2 == 0:
            h = _ffn_layer(h, ffn_norm[layer], *ff_w, c)
            if last:
                h = _final_norm(h, final_norm)
        else:
            h = _moe_layer(h, ffn_norm[layer], moe_w_router[c], moe_b_router[c], *moe_w, c,
                           final_norm if last else None)
    return h.reshape(batch, seq, d)


def _final_norm_kernel(h_ref, g_ref, o_ref):
    o_ref[...] = _rms(h_ref[...], g_ref[...])


def _final_norm(h, g, tm=1024):
    t, d = h.shape
    return pl.pallas_call(
        _final_norm_kernel,
        grid=(t // tm,),
        in_specs=[pl.BlockSpec((tm, d), lambda i: (i, 0)), _resident((1, d))],
        out_specs=pl.BlockSpec((tm, d), lambda i: (i, 0)),
        out_shape=jax.ShapeDtypeStruct((t, d), F32),
        compiler_params=_params(("arbitrary",)),
        name="final_norm",
    )(h, g.reshape(1, d))
```

```python
import functools

import jax
import jax.numpy as jnp
from jax import lax
from jax.experimental import pallas as pl
from jax.experimental.pallas import tpu as pltpu

F32 = jnp.float32
BF16 = jnp.bfloat16

NORM_EPS = 1e-6
CHUNK = 128
LANES = 128
MEM_HEADS = 4
MEM_HEAD_DIM = 128
MEM_WIDTH = MEM_HEADS * MEM_HEAD_DIM
GM_GROUPS = 8
GM_GROUP_DIM = 128
GM_WIDTH = GM_GROUPS * GM_GROUP_DIM
MLA_HEADS = 8
MLA_NOPE = 128
MLA_ROPE = 64
MLA_V = 128
MLA_Q_LORA = 768
MLA_KV_LORA = 256
MLA_QK_PAD = 256
ROPE_THETA = 10000.0
ML_HEADS = 4
ML_DV = 256
ML_DK = 128
ML_AUG = ML_DV + LANES
CONV_K = 4
N_EXPERTS = 8
TOP_K = 2

VMEM_LIMIT = 56 * 1024 * 1024

NT_DIMS = (((1,), (1,)), ((), ()))


def _params(semantics, vmem=VMEM_LIMIT):
    return pltpu.CompilerParams(dimension_semantics=semantics, vmem_limit_bytes=vmem)


def _resident(shape, index=None):
    index = (0,) * len(shape) if index is None else index
    return pl.BlockSpec(shape, lambda *_: index, pipeline_mode=pl.Buffered(1))


def _dot(a, b):
    return jnp.dot(a, b, preferred_element_type=F32)


def _dot_nt(a, b):
    return lax.dot_general(a, b, NT_DIMS, preferred_element_type=F32)


def _rms(x, g):
    return x * lax.rsqrt(jnp.mean(x * x, axis=-1, keepdims=True) + NORM_EPS) * g


def _silu(x):
    return x * jax.nn.sigmoid(x)


def _gelu(x):
    return 0.5 * x * (1.0 + lax.erf(x * 0.5 ** 0.5))


def _one_hot(mask):
    return jnp.where(mask, 1.0, 0.0).astype(BF16)


def _memkv_kernel(mem_ref, g_ref, w_ref, kv_ref):
    xn = _rms(mem_ref[0], g_ref[0]).astype(BF16)
    kv_ref[0, 0] = _dot(xn, w_ref[0]).astype(BF16)


def _memkv(mem, mem_norm, w_mem_kv):
    b, m, d = mem.shape
    depth = mem_norm.shape[0]
    n = w_mem_kv.shape[-1]
    return pl.pallas_call(
        _memkv_kernel,
        grid=(depth, b),
        in_specs=[
            pl.BlockSpec((1, m, d), lambda l, i: (i, 0, 0)),
            pl.BlockSpec((1, 1, d), lambda l, i: (l, 0, 0)),
            pl.BlockSpec((1, d, n), lambda l, i: (l, 0, 0)),
        ],
        out_specs=pl.BlockSpec((1, 1, m, n), lambda l, i: (l, i, 0, 0)),
        out_shape=jax.ShapeDtypeStruct((depth, b, m, n), BF16),
        compiler_params=_params(("arbitrary", "arbitrary")),
        name="memkv",
    )(mem, mem_norm.reshape(depth, 1, d), w_mem_kv.astype(BF16))


def _mem_attention(q, kv_ref):
    outs = []
    for hd in range(MEM_HEADS):
        lo = hd * MEM_HEAD_DIM
        qh = q[:, lo:lo + MEM_HEAD_DIM].astype(BF16)
        kh = kv_ref[0, 0, :, lo:lo + MEM_HEAD_DIM]
        vh = kv_ref[0, 0, :, MEM_WIDTH + lo:MEM_WIDTH + lo + MEM_HEAD_DIM]
        s = _dot_nt(qh, kh) * MEM_HEAD_DIM ** -0.5
        p = jnp.exp(s - jnp.max(s, axis=-1, keepdims=True))
        outs.append(_dot(p.astype(BF16), vh) / jnp.sum(p, axis=-1, keepdims=True))
    return jnp.concatenate(outs, axis=-1)


def _gmlp_layer_kernel(h_ref, g_ref, win_ref, vg_ref, ws_ref, bs_ref, kv_ref, wout_ref,
                       o_ref, proj_ref, mix_ref):
    tm = h_ref.shape[0]
    x = h_ref[...]
    xn = _rms(x, g_ref[...]).astype(BF16)
    proj_ref[...] = _dot(xn, win_ref[...])
    row = lax.broadcasted_iota(jnp.int32, (CHUNK, CHUNK), 0)
    col = lax.broadcasted_iota(jnp.int32, (CHUNK, CHUNK), 1)
    causal = col <= row
    for g in range(GM_GROUPS):
        lo = g * GM_GROUP_DIM
        u = _gelu(proj_ref[:, lo:lo + GM_GROUP_DIM])
        v = _gelu(proj_ref[:, GM_WIDTH + lo:GM_WIDTH + lo + GM_GROUP_DIM])
        vn = _rms(v, vg_ref[:, lo:lo + GM_GROUP_DIM]).astype(BF16)
        w = jnp.where(causal, ws_ref[g], 0.0).astype(BF16)
        for c in range(tm // CHUNK):
            r = c * CHUNK
            mixed = _dot(w, vn[r:r + CHUNK]) + bs_ref[g]
            mix_ref[r:r + CHUNK, lo:lo + GM_GROUP_DIM] = (u[r:r + CHUNK] * mixed).astype(BF16)
    mem_o = _mem_attention(proj_ref[:, 2 * GM_WIDTH:2 * GM_WIDTH + MEM_WIDTH], kv_ref)
    mix_ref[:, GM_WIDTH:GM_WIDTH + MEM_WIDTH] = mem_o.astype(BF16)
    o_ref[...] = x + _dot(mix_ref[...], wout_ref[...])


def _gmlp_layer(h, seq, layer, norm_g, w_in, v_norm, w_s, b_s, kv_all, w_out, tm=512):
    t, d = h.shape
    n_in = w_in.shape[1]
    n_mix = w_out.shape[0]
    tiles_per_seq = seq // tm
    m, kvw = kv_all.shape[2], kv_all.shape[3]
    bs_full = jnp.broadcast_to(b_s[:, :, None], (GM_GROUPS, CHUNK, GM_GROUP_DIM)).astype(F32)
    return pl.pallas_call(
        _gmlp_layer_kernel,
        grid=(t // tm,),
        in_specs=[
            pl.BlockSpec((tm, d), lambda i: (i, 0)),
            _resident((1, d)),
            _resident((d, n_in)),
            _resident((1, GM_WIDTH)),
            _resident((GM_GROUPS, CHUNK, CHUNK)),
            _resident((GM_GROUPS, CHUNK, GM_GROUP_DIM)),
            pl.BlockSpec((1, 1, m, kvw), lambda i: (layer, i // tiles_per_seq, 0, 0)),
            _resident((n_mix, d)),
        ],
        out_specs=pl.BlockSpec((tm, d), lambda i: (i, 0)),
        out_shape=jax.ShapeDtypeStruct((t, d), F32),
        scratch_shapes=[pltpu.VMEM((tm, n_in), F32), pltpu.VMEM((tm, n_mix), BF16)],
        compiler_params=_params(("arbitrary",)),
        name="gmlp_layer",
    )(h, norm_g.reshape(1, d), w_in.astype(BF16), v_norm.reshape(1, GM_WIDTH), w_s,
      bs_full, kv_all, w_out.astype(BF16))


FFN_COL_BLOCK = 256


def _ffn_kernel(h_ref, g_ref, wg_ref, wu_ref, wd_ref, o_ref, hid_ref):
    x = h_ref[...]
    xn = _rms(x, g_ref[...]).astype(BF16)
    for lo in range(0, wg_ref.shape[-1], FFN_COL_BLOCK):
        gate = _dot(xn, wg_ref[0, :, lo:lo + FFN_COL_BLOCK])
        up = _dot(xn, wu_ref[0, :, lo:lo + FFN_COL_BLOCK])
        hid_ref[:, lo:lo + FFN_COL_BLOCK] = (_silu(gate) * up).astype(BF16)
    o_ref[...] = x + _dot(hid_ref[...], wd_ref[0])


def _ffn_layer(h, norm_g, w_gate, w_up, w_down, c, tm=512):
    t, d = h.shape
    f = w_gate.shape[-1]
    return pl.pallas_call(
        _ffn_kernel,
        grid=(t // tm,),
        in_specs=[
            pl.BlockSpec((tm, d), lambda i: (i, 0)),
            _resident((1, d)),
            _resident((1, d, f), (c, 0, 0)),
            _resident((1, d, f), (c, 0, 0)),
            _resident((1, f, d), (c, 0, 0)),
        ],
        out_specs=pl.BlockSpec((tm, d), lambda i: (i, 0)),
        out_shape=jax.ShapeDtypeStruct((t, d), F32),
        scratch_shapes=[pltpu.VMEM((tm, f), BF16)],
        compiler_params=_params(("arbitrary",)),
        name="ffn_layer",
    )(h, norm_g.reshape(1, d), w_gate, w_up, w_down)


MOE_SLOT_BLOCK = 256
MOE_SLOT_VARIANTS = (256, 288, 320)


def _split_bf16(x):
    hi = x.astype(BF16)
    return hi, (x - hi.astype(F32)).astype(BF16)


def _top2_gates(logits, lane):
    m1 = jnp.max(logits, axis=-1, keepdims=True)
    i1 = jnp.min(jnp.where(logits == m1, lane, LANES), axis=-1, keepdims=True)
    first = lane == i1
    rest = jnp.where(first, -jnp.inf, logits)
    m2 = jnp.max(rest, axis=-1, keepdims=True)
    i2 = jnp.min(jnp.where(rest == m2, lane, LANES), axis=-1, keepdims=True)
    second = lane == i2
    e2 = jnp.exp(m2 - m1)
    denom = 1.0 + e2
    return jnp.where(first, 1.0 / denom, 0.0) + jnp.where(second, e2 / denom, 0.0)


def _moe_kernel(h_ref, g_ref, wr_ref, br_ref, wg_ref, wu_ref, wd_ref, fg_ref,
                o_ref, xn_ref, grow_ref, gcol_ref, rrow_ref, rcol_ref, *, final_norm):
    e = pl.program_id(1)
    tm, d = h_ref.shape

    @pl.when(e == 0)
    def _():
        x = h_ref[...]
        xn = _rms(x, g_ref[...])
        o_ref[...] = x
        x_hi, x_lo = _split_bf16(xn)
        w_hi, w_lo = _split_bf16(wr_ref[...])
        xn_ref[...] = x_hi
        hi_terms = _dot(x_hi, jnp.concatenate([w_hi, w_lo], axis=1))
        logits = hi_terms[:, :LANES] + hi_terms[:, LANES:] + _dot(x_lo, w_hi) + br_ref[...]
        lane = lax.broadcasted_iota(jnp.int32, (tm, LANES), 1)
        gcol = _top2_gates(jnp.where(lane < N_EXPERTS, logits, -jnp.inf), lane)
        gcol_ref[...] = gcol
        grow_ref[...] = gcol.T[:N_EXPERTS]
        sel = _one_hot(gcol != 0.0)
        r = lax.broadcasted_iota(jnp.int32, (CHUNK, CHUNK), 0)
        c = lax.broadcasted_iota(jnp.int32, (CHUNK, CHUNK), 1)
        earlier = _one_hot(c < r)
        base = jnp.zeros((1, LANES), F32)
        for b in range(tm // CHUNK):
            blk = sel[b * CHUNK:(b + 1) * CHUNK]
            rcol_ref[b * CHUNK:(b + 1) * CHUNK, :] = _dot(earlier, blk) + base
            base = base + jnp.sum(blk.astype(F32), axis=0, keepdims=True)
        rrow_ref[...] = rcol_ref[...].T[:N_EXPERTS]

    gate_row = grow_ref[pl.ds(e, 1), :]
    rank_row = rrow_ref[pl.ds(e, 1), :]
    count = jnp.sum(jnp.where(gate_row != 0.0, 1.0, 0.0)).astype(jnp.int32)

    def expert_block(cs, first_slot):
        lane = lax.broadcasted_iota(jnp.int32, (tm, LANES), 1)
        gate_col = jnp.sum(jnp.where(lane == e, gcol_ref[...], 0.0), axis=-1, keepdims=True)
        rank_col = jnp.sum(jnp.where(lane == e, rcol_ref[...], 0.0), axis=-1, keepdims=True)
        off = jnp.full((1, 1), first_slot, jnp.int32).astype(F32)
        slot_s = lax.broadcasted_iota(jnp.int32, (cs, tm), 0).astype(F32)
        gather = _one_hot(jnp.logical_and(rank_row - off == slot_s, gate_row != 0.0))
        xg = _dot(gather, xn_ref[...]).astype(BF16)
        hid = (_silu(_dot(xg, wg_ref[0, 0])) * _dot(xg, wu_ref[0, 0])).astype(BF16)
        y = _dot(hid, wd_ref[0, 0]).astype(BF16)
        ks = -(-cs // LANES) * LANES
        if ks != cs:
            y = jnp.concatenate([y, jnp.zeros((ks - cs, d), BF16)], axis=0)
        slot_l = lax.broadcasted_iota(jnp.int32, (tm, ks), 1).astype(F32)
        scatter = _one_hot(jnp.logical_and(rank_col - off == slot_l, gate_col != 0.0))
        o_ref[...] += gate_col * _dot(scatter, y)

    lower = 0
    for cs in MOE_SLOT_VARIANTS:
        @pl.when(jnp.logical_and(count > lower, count <= cs))
        def _(cs=cs):
            expert_block(cs, 0)
        lower = cs

    @pl.when(count > lower)
    def _():
        def body(b, carry):
            expert_block(MOE_SLOT_BLOCK, b * MOE_SLOT_BLOCK)
            return carry
        lax.fori_loop(0, (count + MOE_SLOT_BLOCK - 1) // MOE_SLOT_BLOCK, body, 0)

    if final_norm:
        @pl.when(e == pl.num_programs(1) - 1)
        def _():
            o_ref[...] = _rms(o_ref[...], fg_ref[...])


def _moe_layer(h, norm_g, w_router, b_router, w_gate, w_up, w_down, c, final_g, tm=1024):
    t, d = h.shape
    f = w_gate.shape[-1]
    final_norm = final_g is not None
    fg = (final_g if final_norm else jnp.ones((d,), F32)).reshape(1, d)
    wr = jnp.pad(w_router, ((0, 0), (0, LANES - N_EXPERTS)))
    br = jnp.pad(b_router, (0, LANES - N_EXPERTS)).reshape(1, LANES)
    return pl.pallas_call(
        functools.partial(_moe_kernel, final_norm=final_norm),
        grid=(t // tm, N_EXPERTS),
        in_specs=[
            pl.BlockSpec((tm, d), lambda i, e: (i, 0)),
            _resident((1, d)),
            _resident((d, LANES)),
            _resident((1, LANES)),
            pl.BlockSpec((1, 1, d, f), lambda i, e: (c, e, 0, 0)),
            pl.BlockSpec((1, 1, d, f), lambda i, e: (c, e, 0, 0)),
            pl.BlockSpec((1, 1, f, d), lambda i, e: (c, e, 0, 0)),
            _resident((1, d)),
        ],
        out_specs=pl.BlockSpec((tm, d), lambda i, e: (i, 0)),
        out_shape=jax.ShapeDtypeStruct((t, d), F32),
        scratch_shapes=[
            pltpu.VMEM((tm, d), BF16),
            pltpu.VMEM((N_EXPERTS, tm), F32),
            pltpu.VMEM((tm, LANES), F32),
            pltpu.VMEM((N_EXPERTS, tm), F32),
            pltpu.VMEM((tm, LANES), F32),
        ],
        compiler_params=_params(("arbitrary", "arbitrary")),
        name="moe_layer",
    )(h, norm_g.reshape(1, d), wr, br, w_gate, w_up, w_down, fg)


def _mla_proj_kernel(h_ref, g_ref, pos_ref, freq_ref, win_ref, qg_ref, wuq_ref, kvg_ref, wukv_ref,
                     kv_ref, q_out, k_out, v_out, memo_out):
    xn = _rms(h_ref[...], g_ref[...]).astype(BF16)
    proj = _dot(xn, win_ref[...])
    c_q = proj[:, :MLA_Q_LORA]
    c_kv = proj[:, MLA_Q_LORA:MLA_Q_LORA + MLA_KV_LORA]
    lo = MLA_Q_LORA + MLA_KV_LORA
    mem_q = proj[:, lo:lo + MEM_WIDTH]
    k_rope = proj[:, lo + MEM_WIDTH:lo + MEM_WIDTH + LANES]
    memo_out[...] = _mem_attention(mem_q, kv_ref).astype(BF16)

    ang = pos_ref[...].astype(F32) * freq_ref[...]
    lane = lax.broadcasted_iota(jnp.int32, ang.shape, 1)
    cos = jnp.where(lane < MLA_ROPE, jnp.cos(ang), 0.0)
    sin = jnp.sin(ang)
    sin = jnp.where(lane < MLA_ROPE // 2, -sin, jnp.where(lane < MLA_ROPE, sin, 0.0))

    def rope(r):
        return r * cos + pltpu.roll(r, MLA_ROPE // 2, 1) * sin

    scale = (MLA_NOPE + MLA_ROPE) ** -0.5
    q = _dot(_rms(c_q, qg_ref[...]).astype(BF16), wuq_ref[...])
    kvu = _dot(_rms(c_kv, kvg_ref[...]).astype(BF16), wukv_ref[...])
    k_r = rope(k_rope).astype(BF16)
    for hd in range(MLA_HEADS):
        qlo = hd * MLA_QK_PAD
        q_out[:, qlo:qlo + MLA_NOPE] = (q[:, qlo:qlo + MLA_NOPE] * scale).astype(BF16)
        q_out[:, qlo + MLA_NOPE:qlo + MLA_QK_PAD] = (
            rope(q[:, qlo + MLA_NOPE:qlo + MLA_QK_PAD]) * scale).astype(BF16)
        k_out[:, qlo:qlo + MLA_NOPE] = kvu[:, hd * MLA_NOPE:(hd + 1) * MLA_NOPE].astype(BF16)
        k_out[:, qlo + MLA_NOPE:qlo + MLA_QK_PAD] = k_r
    v_out[...] = kvu[:, MLA_HEADS * MLA_NOPE:].astype(BF16)


def _flash_kernel(qi_ref, kj_ref, q_ref, k_ref, v_ref, h_ref, memo_ref, wa_ref, wm_ref, o_ref,
                  m_ref, acc_ref, attn_ref):
    step_id = pl.program_id(1)
    i = qi_ref[step_id]
    j = kj_ref[step_id]
    tq, tk = q_ref.shape[0], k_ref.shape[0]

    @pl.when(j == 0)
    def _():
        m_ref[...] = jnp.full_like(m_ref, -jnp.inf)
        acc_ref[...] = jnp.zeros_like(acc_ref)

    ones = jnp.ones((tk, MLA_V), BF16)

    def step(masked):
        if masked:
            row = lax.broadcasted_iota(jnp.int32, (tq, tk), 0)
            col = lax.broadcasted_iota(jnp.int32, (tq, tk), 1)
            keep = col <= row
        for hd in range(MLA_HEADS):
            q = q_ref[:, hd * MLA_QK_PAD:(hd + 1) * MLA_QK_PAD]
            k = k_ref[:, hd * MLA_QK_PAD:(hd + 1) * MLA_QK_PAD]
            v_aug = jnp.concatenate([v_ref[:, hd * MLA_V:(hd + 1) * MLA_V], ones], axis=1)
            s = _dot_nt(q, k)
            if masked:
                s = jnp.where(keep, s, -jnp.inf)
            m_prev = m_ref[hd]
            m_new = jnp.maximum(m_prev, jnp.max(s, axis=-1, keepdims=True))
            alpha = jnp.exp(m_prev - m_new)
            p = jnp.exp((s - jnp.tile(m_new, (1, tk // LANES))).astype(BF16))
            acc_ref[hd] = jnp.tile(alpha, (1, 2)) * acc_ref[hd] + _dot(p, v_aug)
            m_ref[hd] = m_new

    @pl.when(j < i)
    def _():
        step(False)

    @pl.when(j == i)
    def _():
        step(True)
        for hd in range(MLA_HEADS):
            acc = acc_ref[hd]
            attn_ref[:, hd * MLA_V:(hd + 1) * MLA_V] = (acc[:, :MLA_V] / acc[:, MLA_V:]).astype(BF16)
        o_ref[...] = (h_ref[...] + _dot(attn_ref[...], wa_ref[...])
                      + _dot(memo_ref[...], wm_ref[...]))


def _mla_layer(h, batch, seq, layer, positions, norm_g, w_in, q_norm, w_uq, kv_norm, w_ukv,
               kv_all, w_out, tm=512, tq=512):
    t, d = h.shape
    half = MLA_ROPE // 2
    lo = MLA_Q_LORA + MLA_KV_LORA
    w_kr = w_in[:, lo:lo + MLA_ROPE]
    w_in_r = jnp.concatenate([w_in[:, :lo], w_in[:, lo + MLA_ROPE:], w_kr, w_kr], axis=1).astype(BF16)
    wq = w_uq.reshape(MLA_Q_LORA, MLA_HEADS, MLA_NOPE + MLA_ROPE)
    wq_r = jnp.concatenate([wq, wq[:, :, MLA_NOPE:]], axis=2).reshape(
        MLA_Q_LORA, MLA_HEADS * MLA_QK_PAD).astype(BF16)
    wkv = w_ukv.reshape(MLA_KV_LORA, MLA_HEADS, MLA_NOPE + MLA_V)
    wkv_r = jnp.concatenate([wkv[:, :, :MLA_NOPE].reshape(MLA_KV_LORA, -1),
                             wkv[:, :, MLA_NOPE:].reshape(MLA_KV_LORA, -1)], axis=1).astype(BF16)
    inv_freq = 1.0 / (ROPE_THETA ** (jnp.arange(0, MLA_ROPE, 2, dtype=F32) / MLA_ROPE))
    freq = jnp.tile(inv_freq, LANES // half).reshape(1, LANES)
    n_in = w_in_r.shape[1]
    tiles_per_seq = seq // tm
    m, kvw = kv_all.shape[2], kv_all.shape[3]
    qk_w = MLA_HEADS * MLA_QK_PAD
    v_w = MLA_HEADS * MLA_V
    q, k, v, memo = pl.pallas_call(
        _mla_proj_kernel,
        grid=(t // tm,),
        in_specs=[
            pl.BlockSpec((tm, d), lambda i: (i, 0)),
            _resident((1, d)),
            pl.BlockSpec((tm, 1), lambda i: (i, 0)),
            _resident((1, LANES)),
            _resident((d, n_in)),
            _resident((1, MLA_Q_LORA)),
            _resident((MLA_Q_LORA, qk_w)),
            _resident((1, MLA_KV_LORA)),
            _resident((MLA_KV_LORA, 2 * v_w)),
            pl.BlockSpec((1, 1, m, kvw), lambda i: (layer, i // tiles_per_seq, 0, 0)),
        ],
        out_specs=[
            pl.BlockSpec((tm, qk_w), lambda i: (i, 0)),
            pl.BlockSpec((tm, qk_w), lambda i: (i, 0)),
            pl.BlockSpec((tm, v_w), lambda i: (i, 0)),
            pl.BlockSpec((tm, MEM_WIDTH), lambda i: (i, 0)),
        ],
        out_shape=[
            jax.ShapeDtypeStruct((t, qk_w), BF16),
            jax.ShapeDtypeStruct((t, qk_w), BF16),
            jax.ShapeDtypeStruct((t, v_w), BF16),
            jax.ShapeDtypeStruct((t, MEM_WIDTH), BF16),
        ],
        compiler_params=_params(("arbitrary",)),
        name="mla_proj",
    )(h, norm_g.reshape(1, d), positions.reshape(t, 1), freq, w_in_r, q_norm.reshape(1, -1), wq_r,
      kv_norm.reshape(1, -1), wkv_r, kv_all)

    nq = seq // tq
    pairs = [(i, j) for i in range(nq) for j in range(i + 1)]
    qi = jnp.array([p[0] for p in pairs], jnp.int32)
    kj = jnp.array([p[1] for p in pairs], jnp.int32)
    w_out_b = w_out.astype(BF16)
    q_tile = lambda b, s, qi, kj: (b * nq + qi[s], 0)
    k_tile = lambda b, s, qi, kj: (b * nq + kj[s], 0)
    return pl.pallas_call(
        _flash_kernel,
        grid_spec=pltpu.PrefetchScalarGridSpec(
            num_scalar_prefetch=2,
            grid=(batch, len(pairs)),
            in_specs=[
                pl.BlockSpec((tq, qk_w), q_tile),
                pl.BlockSpec((tq, qk_w), k_tile),
                pl.BlockSpec((tq, v_w), k_tile),
                pl.BlockSpec((tq, d), q_tile),
                pl.BlockSpec((tq, MEM_WIDTH), q_tile),
                _resident((v_w, d)),
                _resident((MEM_WIDTH, d)),
            ],
            out_specs=pl.BlockSpec((tq, d), q_tile),
            scratch_shapes=[
                pltpu.VMEM((MLA_HEADS, tq, LANES), F32),
                pltpu.VMEM((MLA_HEADS, tq, 2 * MLA_V), F32),
                pltpu.VMEM((tq, v_w), BF16),
            ],
        ),
        out_shape=jax.ShapeDtypeStruct((t, d), F32),
        compiler_params=_params(("arbitrary", "arbitrary")),
        name="mla_flash",
    )(qi, kj, q, k, v, h, memo, w_out_b[:v_w], w_out_b[v_w:])


def _mlstm_layer_kernel(h_ref, g_ref, win_ref, cw_ref, cb_ref, gb_ref, hg_ref, kv_ref, wout_ref,
                        o_ref, qk_ref, c_ref, m_ref, mix_ref, *, ck):
    tm = h_ref.shape[0]
    qk_w = 2 * ML_HEADS * ML_DK
    hv = ML_HEADS * ML_DV
    pad = 8

    @pl.when(pl.program_id(1) == 0)
    def _():
        qk_ref[0:pad, :] = jnp.zeros((pad, qk_w), F32)
        c_ref[...] = jnp.zeros_like(c_ref)
        m_ref[...] = jnp.zeros_like(m_ref)

    x = h_ref[...]
    xn = _rms(x, g_ref[...]).astype(BF16)
    proj = _dot(xn, win_ref[...])
    qk_ref[pad:pad + tm, :] = proj[:, :qk_w]
    conv = cb_ref[...]
    for j in range(CONV_K):
        conv = conv + cw_ref[j:j + 1, :] * qk_ref[pad - (CONV_K - 1) + j:pad - (CONV_K - 1) + j + tm, :]
    qk_ref[0:pad, :] = qk_ref[tm:tm + pad, :]
    qk = _silu(conv)
    v_all = proj[:, qk_w:qk_w + hv]
    o_all = proj[:, qk_w + hv:qk_w + 2 * hv]
    mem_q = proj[:, qk_w + 2 * hv:qk_w + 2 * hv + MEM_WIDTH]
    gates = proj[:, qk_w + 2 * hv + MEM_WIDTH:] + gb_ref[...]

    lane = lax.broadcasted_iota(jnp.int32, (ck, LANES), 1)
    row = lax.broadcasted_iota(jnp.int32, (ck, ck), 0)
    col = lax.broadcasted_iota(jnp.int32, (ck, ck), 1)
    causal = col <= row
    tril = _one_hot(causal)
    ones_blk = _one_hot(lane == 0)

    for c in range(tm // ck):
        r0 = c * ck
        gc = gates[r0:r0 + ck]
        logs = jnp.where(lane >= ML_HEADS, jax.nn.log_sigmoid(gc), gc)
        hi = logs.astype(BF16)
        r1 = logs - hi.astype(F32)
        mid = r1.astype(BF16)
        low = (r1 - mid.astype(F32)).astype(BF16)
        cum = _dot(tril, hi) + _dot(tril, mid) + _dot(tril, low)
        col_vals = jnp.where(lane >= ML_HEADS, cum, logs)
        row_vals = col_vals.T
        for hd in range(ML_HEADS):
            q = qk[r0:r0 + ck, hd * ML_DK:(hd + 1) * ML_DK].astype(BF16)
            k32 = qk[r0:r0 + ck, qk_w // 2 + hd * ML_DK:qk_w // 2 + (hd + 1) * ML_DK] * ML_DK ** -0.5
            v = v_all[r0:r0 + ck, hd * ML_DV:(hd + 1) * ML_DV].astype(BF16)
            v_aug = jnp.concatenate([v, ones_blk], axis=1)
            a_col = col_vals[:, ML_HEADS + hd:ML_HEADS + hd + 1]
            li_col = col_vals[:, hd:hd + 1]
            b_row = row_vals[ML_HEADS + hd:ML_HEADS + hd + 1, :]
            li_row = row_vals[hd:hd + 1, :]
            m_in = m_ref[hd]
            c_in = c_ref[hd]

            log_d = jnp.where(causal, a_col - b_row + li_row, -jnp.inf)
            m_intra = jnp.max(log_d, axis=-1, keepdims=True)
            log_inter = a_col + m_in
            m_t = jnp.maximum(log_inter, m_intra)
            inter = jnp.exp(log_inter - m_t)
            p = (_dot_nt(q, k32.astype(BF16)) * jnp.exp(log_d - m_t)).astype(BF16)
            numden = inter * _dot(q, c_in.astype(BF16)) + _dot(p, v_aug)
            den = numden[:, ML_DV:ML_DV + 1]
            h_out = numden[:, :ML_DV] / jnp.maximum(jnp.abs(den), jnp.exp(-m_t))
            h_n = _rms(h_out, hg_ref[:, hd * ML_DV:(hd + 1) * ML_DV])
            gate_o = jax.nn.sigmoid(o_all[r0:r0 + ck, hd * ML_DV:(hd + 1) * ML_DV])
            mix_ref[r0:r0 + ck, hd * ML_DV:(hd + 1) * ML_DV] = (h_n * gate_o).astype(BF16)

            f_tot = b_row[:, ck - 1:ck]
            log_w = f_tot - a_col + li_col
            m_loc = jnp.max(log_w, axis=0, keepdims=True)
            kw_t = (k32 * jnp.exp(log_w - m_loc)).T.astype(BF16)
            c_loc = _dot(kw_t, v_aug)
            m_new = jnp.maximum(f_tot + m_in, m_loc)
            c_ref[hd] = jnp.exp(f_tot + m_in - m_new) * c_in + jnp.exp(m_loc - m_new) * c_loc
            m_ref[hd] = m_new

    mix_ref[:, hv:hv + MEM_WIDTH] = _mem_attention(mem_q, kv_ref).astype(BF16)
    o_ref[...] = x + _dot(mix_ref[...], wout_ref[...])


def _mlstm_layer(h, batch, seq, layer, norm_g, w_in, conv_w, conv_b, gate_b, h_norm, kv_all, w_out,
                 tm=512, ck=256):
    t, d = h.shape
    qk_w = 2 * ML_HEADS * ML_DK
    hv = ML_HEADS * ML_DV
    n_gate = 2 * ML_HEADS
    lo = qk_w + 2 * hv
    w_in_r = jnp.concatenate(
        [w_in[:, :lo], w_in[:, lo + n_gate:], w_in[:, lo:lo + n_gate],
         jnp.zeros((d, LANES - n_gate), w_in.dtype)], axis=1).astype(BF16)
    gb = jnp.pad(gate_b, (0, LANES - n_gate)).reshape(1, LANES)
    n_in = w_in_r.shape[1]
    n_mix = w_out.shape[0]
    tiles = seq // tm
    m, kvw = kv_all.shape[2], kv_all.shape[3]
    return pl.pallas_call(
        functools.partial(_mlstm_layer_kernel, ck=ck),
        grid=(batch, tiles),
        in_specs=[
            pl.BlockSpec((tm, d), lambda b, i: (b * tiles + i, 0)),
            _resident((1, d)),
            _resident((d, n_in)),
            _resident((CONV_K, qk_w)),
            _resident((1, qk_w)),
            _resident((1, LANES)),
            _resident((1, hv)),
            pl.BlockSpec((1, 1, m, kvw), lambda b, i: (layer, b, 0, 0)),
            _resident((n_mix, d)),
        ],
        out_specs=pl.BlockSpec((tm, d), lambda b, i: (b * tiles + i, 0)),
        out_shape=jax.ShapeDtypeStruct((t, d), F32),
        scratch_shapes=[
            pltpu.VMEM((tm + 8, qk_w), F32),
            pltpu.VMEM((ML_HEADS, ML_DK, ML_AUG), F32),
            pltpu.VMEM((ML_HEADS, 1, 1), F32),
            pltpu.VMEM((tm, n_mix), BF16),
        ],
        compiler_params=_params(("arbitrary", "arbitrary")),
        name="mlstm_layer",
    )(h, norm_g.reshape(1, d), w_in_r, conv_w, conv_b.reshape(1, qk_w), gb, h_norm.reshape(1, hv),
      kv_all, w_out.astype(BF16))


def kernel(x, mem, positions, attn_norm, mem_norm, w_mem_kv, ffn_norm, final_norm, gm_w_in, gm_v_norm, gm_w_s, gm_b_s, gm_w_out, mla_w_in, mla_q_norm, mla_w_uq, mla_kv_norm, mla_w_ukv, mla_w_out, ml_w_in, ml_conv_w, ml_conv_b, ml_gate_b, ml_h_norm, ml_w_out, ff_w_gate, ff_w_up, ff_w_down, moe_w_router, moe_b_router, moe_w_gate, moe_w_up, moe_w_down):
    batch, seq, d = x.shape
    depth = attn_norm.shape[0]
    kv_all = _memkv(mem, mem_norm, w_mem_kv)
    ff_w = [w.astype(BF16) for w in (ff_w_gate, ff_w_up, ff_w_down)]
    moe_w = [w.astype(BF16) for w in (moe_w_gate, moe_w_up, moe_w_down)]
    h = x.reshape(batch * seq, d)
    for layer in range(depth):
        kind, j = layer % 3, layer // 3
        c = layer // 2
        last = layer == depth - 1
        if kind == 0:
            h = _gmlp_layer(h, seq, layer, attn_norm[layer], gm_w_in[j], gm_v_norm[j], gm_w_s[j],
                            gm_b_s[j], kv_all, gm_w_out[j])
        elif kind == 1:
            h = _mla_layer(h, batch, seq, layer, positions, attn_norm[layer], mla_w_in[j],
                           mla_q_norm[j], mla_w_uq[j], mla_kv_norm[j], mla_w_ukv[j], kv_all,
                           mla_w_out[j])
        else:
            h = _mlstm_layer(h, batch, seq, layer, attn_norm[layer], ml_w_in[j], ml_conv_w[j],
                             ml_conv_b[j], ml_gate_b[j], ml_h_norm[j], kv_all, ml_w_out[j])
        if layer % 2 == 0:
            h = _ffn_layer(h, ffn_norm[layer], *ff_w, c)
            if last:
                h = _final_norm(h, final_norm)
        else:
            h = _moe_layer(h, ffn_norm[layer], moe_w_router[c], moe_b_router[c], *moe_w, c,
                           final_norm if last else None)
    return h.reshape(batch, seq, d)


def _final_norm_kernel(h_ref, g_ref, o_ref):
    o_ref[...] = _rms(h_ref[...], g_ref[...])


def _final_norm(h, g, tm=1024):
    t, d = h.shape
    return pl.pallas_call(
        _final_norm_kernel,
        grid=(t // tm,),
        in_specs=[pl.BlockSpec((tm, d), lambda i: (i, 0)), _resident((1, d))],
        out_specs=pl.BlockSpec((tm, d), lambda i: (i, 0)),
        out_shape=jax.ShapeDtypeStruct((t, d), F32),
        compiler_params=_params(("arbitrary",)),
        name="final_norm",
    )(h, g.reshape(1, d))
```

```python
import functools

import jax
import jax.numpy as jnp
from jax import lax
from jax.experimental import pallas as pl
from jax.experimental.pallas import tpu as pltpu

F32 = jnp.float32
BF16 = jnp.bfloat16

NORM_EPS = 1e-6
CHUNK = 128
LANES = 128
MEM_HEADS = 4
MEM_HEAD_DIM = 128
MEM_WIDTH = MEM_HEADS * MEM_HEAD_DIM
GM_GROUPS = 8
GM_GROUP_DIM = 128
GM_WIDTH = GM_GROUPS * GM_GROUP_DIM
MLA_HEADS = 8
MLA_NOPE = 128
MLA_ROPE = 64
MLA_V = 128
MLA_Q_LORA = 768
MLA_KV_LORA = 256
MLA_QK_PAD = 256
ROPE_THETA = 10000.0
ML_HEADS = 4
ML_DV = 256
ML_DK = 128
ML_AUG = ML_DV + LANES
CONV_K = 4
N_EXPERTS = 8
TOP_K = 2

VMEM_LIMIT = 56 * 1024 * 1024

NT_DIMS = (((1,), (1,)), ((), ()))


def _params(semantics, vmem=VMEM_LIMIT):
    return pltpu.CompilerParams(dimension_semantics=semantics, vmem_limit_bytes=vmem)


def _resident(shape, index=None):
    index = (0,) * len(shape) if index is None else index
    return pl.BlockSpec(shape, lambda *_: index, pipeline_mode=pl.Buffered(1))


def _dot(a, b):
    return jnp.dot(a, b, preferred_element_type=F32)


def _dot_nt(a, b):
    return lax.dot_general(a, b, NT_DIMS, preferred_element_type=F32)


def _rms(x, g):
    return x * lax.rsqrt(jnp.mean(x * x, axis=-1, keepdims=True) + NORM_EPS) * g


def _silu(x):
    return x * jax.nn.sigmoid(x)


def _gelu(x):
    return 0.5 * x * (1.0 + lax.erf(x * 0.5 ** 0.5))


def _one_hot(mask):
    return jnp.where(mask, 1.0, 0.0).astype(BF16)


def _memkv_kernel(mem_ref, g_ref, w_ref, kv_ref):
    xn = _rms(mem_ref[0], g_ref[0]).astype(BF16)
    kv_ref[0, 0] = _dot(xn, w_ref[0]).astype(BF16)


def _memkv(mem, mem_norm, w_mem_kv):
    b, m, d = mem.shape
    depth = mem_norm.shape[0]
    n = w_mem_kv.shape[-1]
    return pl.pallas_call(
        _memkv_kernel,
        grid=(depth, b),
        in_specs=[
            pl.BlockSpec((1, m, d), lambda l, i: (i, 0, 0)),
            pl.BlockSpec((1, 1, d), lambda l, i: (l, 0, 0)),
            pl.BlockSpec((1, d, n), lambda l, i: (l, 0, 0)),
        ],
        out_specs=pl.BlockSpec((1, 1, m, n), lambda l, i: (l, i, 0, 0)),
        out_shape=jax.ShapeDtypeStruct((depth, b, m, n), BF16),
        compiler_params=_params(("arbitrary", "arbitrary")),
        name="memkv",
    )(mem, mem_norm.reshape(depth, 1, d), w_mem_kv.astype(BF16))


def _mem_attention(q, kv_ref):
    outs = []
    for hd in range(MEM_HEADS):
        lo = hd * MEM_HEAD_DIM
        qh = q[:, lo:lo + MEM_HEAD_DIM].astype(BF16)
        kh = kv_ref[0, 0, :, lo:lo + MEM_HEAD_DIM]
        vh = kv_ref[0, 0, :, MEM_WIDTH + lo:MEM_WIDTH + lo + MEM_HEAD_DIM]
        s = _dot_nt(qh, kh) * MEM_HEAD_DIM ** -0.5
        p = jnp.exp(s - jnp.max(s, axis=-1, keepdims=True))
        outs.append(_dot(p.astype(BF16), vh) / jnp.sum(p, axis=-1, keepdims=True))
    return jnp.concatenate(outs, axis=-1)


def _gmlp_layer_kernel(h_ref, g_ref, win_ref, vg_ref, ws_ref, bs_ref, kv_ref, wout_ref,
                       o_ref, proj_ref, mix_ref):
    tm = h_ref.shape[0]
    x = h_ref[...]
    xn = _rms(x, g_ref[...]).astype(BF16)
    proj_ref[...] = _dot(xn, win_ref[...])
    row = lax.broadcasted_iota(jnp.int32, (CHUNK, CHUNK), 0)
    col = lax.broadcasted_iota(jnp.int32, (CHUNK, CHUNK), 1)
    causal = col <= row
    for g in range(GM_GROUPS):
        lo = g * GM_GROUP_DIM
        u = _gelu(proj_ref[:, lo:lo + GM_GROUP_DIM])
        v = _gelu(proj_ref[:, GM_WIDTH + lo:GM_WIDTH + lo + GM_GROUP_DIM])
        vn = _rms(v, vg_ref[:, lo:lo + GM_GROUP_DIM]).astype(BF16)
        w = jnp.where(causal, ws_ref[g], 0.0).astype(BF16)
        for c in range(tm // CHUNK):
            r = c * CHUNK
            mixed = _dot(w, vn[r:r + CHUNK]) + bs_ref[g]
            mix_ref[r:r + CHUNK, lo:lo + GM_GROUP_DIM] = (u[r:r + CHUNK] * mixed).astype(BF16)
    mem_o = _mem_attention(proj_ref[:, 2 * GM_WIDTH:2 * GM_WIDTH + MEM_WIDTH], kv_ref)
    mix_ref[:, GM_WIDTH:GM_WIDTH + MEM_WIDTH] = mem_o.astype(BF16)
    o_ref[...] = x + _dot(mix_ref[...], wout_ref[...])


def _gmlp_layer(h, seq, layer, norm_g, w_in, v_norm, w_s, b_s, kv_all, w_out, tm=512):
    t, d = h.shape
    n_in = w_in.shape[1]
    n_mix = w_out.shape[0]
    tiles_per_seq = seq // tm
    m, kvw = kv_all.shape[2], kv_all.shape[3]
    bs_full = jnp.broadcast_to(b_s[:, :, None], (GM_GROUPS, CHUNK, GM_GROUP_DIM)).astype(F32)
    return pl.pallas_call(
        _gmlp_layer_kernel,
        grid=(t // tm,),
        in_specs=[
            pl.BlockSpec((tm, d), lambda i: (i, 0)),
            _resident((1, d)),
            _resident((d, n_in)),
            _resident((1, GM_WIDTH)),
            _resident((GM_GROUPS, CHUNK, CHUNK)),
            _resident((GM_GROUPS, CHUNK, GM_GROUP_DIM)),
            pl.BlockSpec((1, 1, m, kvw), lambda i: (layer, i // tiles_per_seq, 0, 0)),
            _resident((n_mix, d)),
        ],
        out_specs=pl.BlockSpec((tm, d), lambda i: (i, 0)),
        out_shape=jax.ShapeDtypeStruct((t, d), F32),
        scratch_shapes=[pltpu.VMEM((tm, n_in), F32), pltpu.VMEM((tm, n_mix), BF16)],
        compiler_params=_params(("arbitrary",)),
        name="gmlp_layer",
    )(h, norm_g.reshape(1, d), w_in.astype(BF16), v_norm.reshape(1, GM_WIDTH), w_s,
      bs_full, kv_all, w_out.astype(BF16))


FFN_COL_BLOCK = 256


def _ffn_kernel(h_ref, g_ref, wg_ref, wu_ref, wd_ref, o_ref, hid_ref):
    x = h_ref[...]
    xn = _rms(x, g_ref[...]).astype(BF16)
    for lo in range(0, wg_ref.shape[-1], FFN_COL_BLOCK):
        gate = _dot(xn, wg_ref[0, :, lo:lo + FFN_COL_BLOCK])
        up = _dot(xn, wu_ref[0, :, lo:lo + FFN_COL_BLOCK])
        hid_ref[:, lo:lo + FFN_COL_BLOCK] = (_silu(gate) * up).astype(BF16)
    o_ref[...] = x + _dot(hid_ref[...], wd_ref[0])


def _ffn_layer(h, norm_g, w_gate, w_up, w_down, c, tm=512):
    t, d = h.shape
    f = w_gate.shape[-1]
    return pl.pallas_call(
        _ffn_kernel,
        grid=(t // tm,),
        in_specs=[
            pl.BlockSpec((tm, d), lambda i: (i, 0)),
            _resident((1, d)),
            _resident((1, d, f), (c, 0, 0)),
            _resident((1, d, f), (c, 0, 0)),
            _resident((1, f, d), (c, 0, 0)),
        ],
        out_specs=pl.BlockSpec((tm, d), lambda i: (i, 0)),
        out_shape=jax.ShapeDtypeStruct((t, d), F32),
        scratch_shapes=[pltpu.VMEM((tm, f), BF16)],
        compiler_params=_params(("arbitrary",)),
        name="ffn_layer",
    )(h, norm_g.reshape(1, d), w_gate, w_up, w_down)


MOE_SLOT_BLOCK = 256
MOE_SLOT_VARIANTS = (256, 288, 320)
MOE_ROW_WINDOW = 256


def _split_bf16(x):
    hi = x.astype(BF16)
    return hi, (x - hi.astype(F32)).astype(BF16)


def _top2_gates(logits, lane):
    m1 = jnp.max(logits, axis=-1, keepdims=True)
    i1 = jnp.min(jnp.where(logits == m1, lane, LANES), axis=-1, keepdims=True)
    first = lane == i1
    rest = jnp.where(first, -jnp.inf, logits)
    m2 = jnp.max(rest, axis=-1, keepdims=True)
    i2 = jnp.min(jnp.where(rest == m2, lane, LANES), axis=-1, keepdims=True)
    second = lane == i2
    e2 = jnp.exp(m2 - m1)
    denom = 1.0 + e2
    return jnp.where(first, 1.0 / denom, 0.0) + jnp.where(second, e2 / denom, 0.0)


def _moe_kernel(h_ref, g_ref, wr_ref, br_ref, wg_ref, wu_ref, wd_ref, fg_ref,
                o_ref, xn_ref, grow_ref, gcol_ref, rrow_ref, rcol_ref, *, final_norm):
    e = pl.program_id(1)
    tm, d = h_ref.shape

    @pl.when(e == 0)
    def _():
        x = h_ref[...]
        xn = _rms(x, g_ref[...])
        o_ref[...] = x
        x_hi, x_lo = _split_bf16(xn)
        w_hi, w_lo = _split_bf16(wr_ref[...])
        xn_ref[...] = x_hi
        hi_terms = _dot(x_hi, jnp.concatenate([w_hi, w_lo], axis=1))
        logits = hi_terms[:, :LANES] + hi_terms[:, LANES:] + _dot(x_lo, w_hi) + br_ref[...]
        lane = lax.broadcasted_iota(jnp.int32, (tm, LANES), 1)
        gcol = _top2_gates(jnp.where(lane < N_EXPERTS, logits, -jnp.inf), lane)
        gcol_ref[...] = gcol
        grow_ref[...] = gcol.T[:N_EXPERTS]
        sel = _one_hot(gcol != 0.0)
        r = lax.broadcasted_iota(jnp.int32, (CHUNK, CHUNK), 0)
        c = lax.broadcasted_iota(jnp.int32, (CHUNK, CHUNK), 1)
        earlier = _one_hot(c < r)
        base = jnp.zeros((1, LANES), F32)
        for b in range(tm // CHUNK):
            blk = sel[b * CHUNK:(b + 1) * CHUNK]
            rcol_ref[b * CHUNK:(b + 1) * CHUNK, :] = _dot(earlier, blk) + base
            base = base + jnp.sum(blk.astype(F32), axis=0, keepdims=True)
        rrow_ref[...] = rcol_ref[...].T[:N_EXPERTS]

    gate_row = grow_ref[pl.ds(e, 1), :]
    rank_row = rrow_ref[pl.ds(e, 1), :]
    picked = jnp.where(gate_row != 0.0, 1.0, 0.0)
    token = lax.broadcasted_iota(jnp.int32, (1, tm), 1)
    count_upto = [jnp.sum(jnp.where(token < (q + 1) * MOE_ROW_WINDOW, picked, 0.0)).astype(jnp.int32)
                  for q in range(tm // MOE_ROW_WINDOW)]
    count = count_upto[-1]

    def expert_block(cs, first_slot):
        lane = lax.broadcasted_iota(jnp.int32, (tm, LANES), 1)
        gate_col = jnp.sum(jnp.where(lane == e, gcol_ref[...], 0.0), axis=-1, keepdims=True)
        rank_col = jnp.sum(jnp.where(lane == e, rcol_ref[...], 0.0), axis=-1, keepdims=True)
        off = jnp.full((1, 1), first_slot, jnp.int32).astype(F32)
        slot_s = lax.broadcasted_iota(jnp.int32, (cs, tm), 0).astype(F32)
        gather = _one_hot(jnp.logical_and(rank_row - off == slot_s, gate_row != 0.0))
        xg = _dot(gather, xn_ref[...]).astype(BF16)
        hid = (_silu(_dot(xg, wg_ref[0, 0])) * _dot(xg, wu_ref[0, 0])).astype(BF16)
        y = _dot(hid, wd_ref[0, 0]).astype(BF16)
        main = MOE_SLOT_BLOCK
        slot_l = lax.broadcasted_iota(jnp.int32, (tm, main), 1).astype(F32)
        scatter = _one_hot(jnp.logical_and(rank_col - off == slot_l, gate_col != 0.0))
        o_ref[...] += gate_col * _dot(scatter, y[:main])
        if cs > main:
            y_over = jnp.concatenate([y[main:], jnp.zeros((LANES - (cs - main), d), BF16)], axis=0)
            slot_o = lax.broadcasted_iota(jnp.int32, (MOE_ROW_WINDOW, LANES), 1).astype(F32) + main
            for q in range(tm // MOE_ROW_WINDOW):
                @pl.when(count_upto[q] > main)
                def _(q=q):
                    rows = slice(q * MOE_ROW_WINDOW, (q + 1) * MOE_ROW_WINDOW)
                    put = jnp.logical_and(rank_col[rows] - off == slot_o, gate_col[rows] != 0.0)
                    o_ref[rows, :] += gate_col[rows] * _dot(_one_hot(put), y_over)

    lower = 0
    for cs in MOE_SLOT_VARIANTS:
        @pl.when(jnp.logical_and(count > lower, count <= cs))
        def _(cs=cs):
            expert_block(cs, 0)
        lower = cs

    @pl.when(count > lower)
    def _():
        def body(b, carry):
            expert_block(MOE_SLOT_BLOCK, b * MOE_SLOT_BLOCK)
            return carry
        lax.fori_loop(0, (count + MOE_SLOT_BLOCK - 1) // MOE_SLOT_BLOCK, body, 0)

    if final_norm:
        @pl.when(e == pl.num_programs(1) - 1)
        def _():
            o_ref[...] = _rms(o_ref[...], fg_ref[...])


def _moe_layer(h, norm_g, w_router, b_router, w_gate, w_up, w_down, c, final_g, tm=1024):
    t, d = h.shape
    f = w_gate.shape[-1]
    final_norm = final_g is not None
    fg = (final_g if final_norm else jnp.ones((d,), F32)).reshape(1, d)
    wr = jnp.pad(w_router, ((0, 0), (0, LANES - N_EXPERTS)))
    br = jnp.pad(b_router, (0, LANES - N_EXPERTS)).reshape(1, LANES)
    return pl.pallas_call(
        functools.partial(_moe_kernel, final_norm=final_norm),
        grid=(t // tm, N_EXPERTS),
        in_specs=[
            pl.BlockSpec((tm, d), lambda i, e: (i, 0)),
            _resident((1, d)),
            _resident((d, LANES)),
            _resident((1, LANES)),
            pl.BlockSpec((1, 1, d, f), lambda i, e: (c, e, 0, 0)),
            pl.BlockSpec((1, 1, d, f), lambda i, e: (c, e, 0, 0)),
            pl.BlockSpec((1, 1, f, d), lambda i, e: (c, e, 0, 0)),
            _resident((1, d)),
        ],
        out_specs=pl.BlockSpec((tm, d), lambda i, e: (i, 0)),
        out_shape=jax.ShapeDtypeStruct((t, d), F32),
        scratch_shapes=[
            pltpu.VMEM((tm, d), BF16),
            pltpu.VMEM((N_EXPERTS, tm), F32),
            pltpu.VMEM((tm, LANES), F32),
            pltpu.VMEM((N_EXPERTS, tm), F32),
            pltpu.VMEM((tm, LANES), F32),
        ],
        compiler_params=_params(("arbitrary", "arbitrary")),
        name="moe_layer",
    )(h, norm_g.reshape(1, d), wr, br, w_gate, w_up, w_down, fg)


def _mla_proj_kernel(h_ref, g_ref, pos_ref, freq_ref, win_ref, qg_ref, wuq_ref, kvg_ref, wukv_ref,
                     kv_ref, q_out, k_out, v_out, memo_out):
    xn = _rms(h_ref[...], g_ref[...]).astype(BF16)
    proj = _dot(xn, win_ref[...])
    c_q = proj[:, :MLA_Q_LORA]
    c_kv = proj[:, MLA_Q_LORA:MLA_Q_LORA + MLA_KV_LORA]
    lo = MLA_Q_LORA + MLA_KV_LORA
    mem_q = proj[:, lo:lo + MEM_WIDTH]
    k_rope = proj[:, lo + MEM_WIDTH:lo + MEM_WIDTH + LANES]
    memo_out[...] = _mem_attention(mem_q, kv_ref).astype(BF16)

    ang = pos_ref[...].astype(F32) * freq_ref[...]
    lane = lax.broadcasted_iota(jnp.int32, ang.shape, 1)
    cos = jnp.where(lane < MLA_ROPE, jnp.cos(ang), 0.0)
    sin = jnp.sin(ang)
    sin = jnp.where(lane < MLA_ROPE // 2, -sin, jnp.where(lane < MLA_ROPE, sin, 0.0))

    def rope(r):
        return r * cos + pltpu.roll(r, MLA_ROPE // 2, 1) * sin

    scale = (MLA_NOPE + MLA_ROPE) ** -0.5
    q = _dot(_rms(c_q, qg_ref[...]).astype(BF16), wuq_ref[...])
    kvu = _dot(_rms(c_kv, kvg_ref[...]).astype(BF16), wukv_ref[...])
    k_r = rope(k_rope).astype(BF16)
    for hd in range(MLA_HEADS):
        qlo = hd * MLA_QK_PAD
        q_out[:, qlo:qlo + MLA_NOPE] = (q[:, qlo:qlo + MLA_NOPE] * scale).astype(BF16)
        q_out[:, qlo + MLA_NOPE:qlo + MLA_QK_PAD] = (
            rope(q[:, qlo + MLA_NOPE:qlo + MLA_QK_PAD]) * scale).astype(BF16)
        k_out[:, qlo:qlo + MLA_NOPE] = kvu[:, hd * MLA_NOPE:(hd + 1) * MLA_NOPE].astype(BF16)
        k_out[:, qlo + MLA_NOPE:qlo + MLA_QK_PAD] = k_r
    v_out[...] = kvu[:, MLA_HEADS * MLA_NOPE:].astype(BF16)


def _flash_kernel(qi_ref, kj_ref, q_ref, k_ref, v_ref, h_ref, memo_ref, wa_ref, wm_ref, o_ref,
                  m_ref, acc_ref, attn_ref):
    step_id = pl.program_id(1)
    i = qi_ref[step_id]
    j = kj_ref[step_id]
    tq, tk = q_ref.shape[0], k_ref.shape[0]

    @pl.when(j == 0)
    def _():
        m_ref[...] = jnp.full_like(m_ref, -jnp.inf)
        acc_ref[...] = jnp.zeros_like(acc_ref)

    ones = jnp.ones((tk, MLA_V), BF16)

    def step(masked):
        if masked:
            row = lax.broadcasted_iota(jnp.int32, (tq, tk), 0)
            col = lax.broadcasted_iota(jnp.int32, (tq, tk), 1)
            keep = col <= row
        for hd in range(MLA_HEADS):
            q = q_ref[:, hd * MLA_QK_PAD:(hd + 1) * MLA_QK_PAD]
            k = k_ref[:, hd * MLA_QK_PAD:(hd + 1) * MLA_QK_PAD]
            v_aug = jnp.concatenate([v_ref[:, hd * MLA_V:(hd + 1) * MLA_V], ones], axis=1)
            s = _dot_nt(q, k)
            if masked:
                s = jnp.where(keep, s, -jnp.inf)
            m_prev = m_ref[hd]
            m_new = jnp.maximum(m_prev, jnp.max(s, axis=-1, keepdims=True))
            alpha = jnp.exp(m_prev - m_new)
            p = jnp.exp((s - jnp.tile(m_new, (1, tk // LANES))).astype(BF16))
            acc_ref[hd] = jnp.tile(alpha, (1, 2)) * acc_ref[hd] + _dot(p, v_aug)
            m_ref[hd] = m_new

    @pl.when(j < i)
    def _():
        step(False)

    @pl.when(j == i)
    def _():
        step(True)
        for hd in range(MLA_HEADS):
            acc = acc_ref[hd]
            attn_ref[:, hd * MLA_V:(hd + 1) * MLA_V] = (acc[:, :MLA_V] / acc[:, MLA_V:]).astype(BF16)
        o_ref[...] = (h_ref[...] + _dot(attn_ref[...], wa_ref[...])
                      + _dot(memo_ref[...], wm_ref[...]))


def _mla_layer(h, batch, seq, layer, positions, norm_g, w_in, q_norm, w_uq, kv_norm, w_ukv,
               kv_all, w_out, tm=512, tq=512):
    t, d = h.shape
    half = MLA_ROPE // 2
    lo = MLA_Q_LORA + MLA_KV_LORA
    w_kr = w_in[:, lo:lo + MLA_ROPE]
    w_in_r = jnp.concatenate([w_in[:, :lo], w_in[:, lo + MLA_ROPE:], w_kr, w_kr], axis=1).astype(BF16)
    wq = w_uq.reshape(MLA_Q_LORA, MLA_HEADS, MLA_NOPE + MLA_ROPE)
    wq_r = jnp.concatenate([wq, wq[:, :, MLA_NOPE:]], axis=2).reshape(
        MLA_Q_LORA, MLA_HEADS * MLA_QK_PAD).astype(BF16)
    wkv = w_ukv.reshape(MLA_KV_LORA, MLA_HEADS, MLA_NOPE + MLA_V)
    wkv_r = jnp.concatenate([wkv[:, :, :MLA_NOPE].reshape(MLA_KV_LORA, -1),
                             wkv[:, :, MLA_NOPE:].reshape(MLA_KV_LORA, -1)], axis=1).astype(BF16)
    inv_freq = 1.0 / (ROPE_THETA ** (jnp.arange(0, MLA_ROPE, 2, dtype=F32) / MLA_ROPE))
    freq = jnp.tile(inv_freq, LANES // half).reshape(1, LANES)
    n_in = w_in_r.shape[1]
    tiles_per_seq = seq // tm
    m, kvw = kv_all.shape[2], kv_all.shape[3]
    qk_w = MLA_HEADS * MLA_QK_PAD
    v_w = MLA_HEADS * MLA_V
    q, k, v, memo = pl.pallas_call(
        _mla_proj_kernel,
        grid=(t // tm,),
        in_specs=[
            pl.BlockSpec((tm, d), lambda i: (i, 0)),
            _resident((1, d)),
            pl.BlockSpec((tm, 1), lambda i: (i, 0)),
            _resident((1, LANES)),
            _resident((d, n_in)),
            _resident((1, MLA_Q_LORA)),
            _resident((MLA_Q_LORA, qk_w)),
            _resident((1, MLA_KV_LORA)),
            _resident((MLA_KV_LORA, 2 * v_w)),
            pl.BlockSpec((1, 1, m, kvw), lambda i: (layer, i // tiles_per_seq, 0, 0)),
        ],
        out_specs=[
            pl.BlockSpec((tm, qk_w), lambda i: (i, 0)),
            pl.BlockSpec((tm, qk_w), lambda i: (i, 0)),
            pl.BlockSpec((tm, v_w), lambda i: (i, 0)),
            pl.BlockSpec((tm, MEM_WIDTH), lambda i: (i, 0)),
        ],
        out_shape=[
            jax.ShapeDtypeStruct((t, qk_w), BF16),
            jax.ShapeDtypeStruct((t, qk_w), BF16),
            jax.ShapeDtypeStruct((t, v_w), BF16),
            jax.ShapeDtypeStruct((t, MEM_WIDTH), BF16),
        ],
        compiler_params=_params(("arbitrary",)),
        name="mla_proj",
    )(h, norm_g.reshape(1, d), positions.reshape(t, 1), freq, w_in_r, q_norm.reshape(1, -1), wq_r,
      kv_norm.reshape(1, -1), wkv_r, kv_all)

    nq = seq // tq
    pairs = [(i, j) for i in range(nq) for j in range(i + 1)]
    qi = jnp.array([p[0] for p in pairs], jnp.int32)
    kj = jnp.array([p[1] for p in pairs], jnp.int32)
    w_out_b = w_out.astype(BF16)
    q_tile = lambda b, s, qi, kj: (b * nq + qi[s], 0)
    k_tile = lambda b, s, qi, kj: (b * nq + kj[s], 0)
    return pl.pallas_call(
        _flash_kernel,
        grid_spec=pltpu.PrefetchScalarGridSpec(
            num_scalar_prefetch=2,
            grid=(batch, len(pairs)),
            in_specs=[
                pl.BlockSpec((tq, qk_w), q_tile),
                pl.BlockSpec((tq, qk_w), k_tile),
                pl.BlockSpec((tq, v_w), k_tile),
                pl.BlockSpec((tq, d), q_tile),
                pl.BlockSpec((tq, MEM_WIDTH), q_tile),
                _resident((v_w, d)),
                _resident((MEM_WIDTH, d)),
            ],
            out_specs=pl.BlockSpec((tq, d), q_tile),
            scratch_shapes=[
                pltpu.VMEM((MLA_HEADS, tq, LANES), F32),
                pltpu.VMEM((MLA_HEADS, tq, 2 * MLA_V), F32),
                pltpu.VMEM((tq, v_w), BF16),
            ],
        ),
        out_shape=jax.ShapeDtypeStruct((t, d), F32),
        compiler_params=_params(("arbitrary", "arbitrary")),
        name="mla_flash",
    )(qi, kj, q, k, v, h, memo, w_out_b[:v_w], w_out_b[v_w:])


def _mlstm_layer_kernel(h_ref, g_ref, win_ref, cw_ref, cb_ref, gb_ref, hg_ref, kv_ref, wout_ref,
                        o_ref, qk_ref, c_ref, m_ref, mix_ref, *, ck):
    tm = h_ref.shape[0]
    qk_w = 2 * ML_HEADS * ML_DK
    hv = ML_HEADS * ML_DV
    pad = 8

    @pl.when(pl.program_id(1) == 0)
    def _():
        qk_ref[0:pad, :] = jnp.zeros((pad, qk_w), F32)
        c_ref[...] = jnp.zeros_like(c_ref)
        m_ref[...] = jnp.zeros_like(m_ref)

    x = h_ref[...]
    xn = _rms(x, g_ref[...]).astype(BF16)
    proj = _dot(xn, win_ref[...])
    qk_ref[pad:pad + tm, :] = proj[:, :qk_w]
    conv = cb_ref[...]
    for j in range(CONV_K):
        conv = conv + cw_ref[j:j + 1, :] * qk_ref[pad - (CONV_K - 1) + j:pad - (CONV_K - 1) + j + tm, :]
    qk_ref[0:pad, :] = qk_ref[tm:tm + pad, :]
    qk = _silu(conv)
    v_all = proj[:, qk_w:qk_w + hv]
    o_all = proj[:, qk_w + hv:qk_w + 2 * hv]
    mem_q = proj[:, qk_w + 2 * hv:qk_w + 2 * hv + MEM_WIDTH]
    gates = proj[:, qk_w + 2 * hv + MEM_WIDTH:] + gb_ref[...]

    lane = lax.broadcasted_iota(jnp.int32, (ck, LANES), 1)
    row = lax.broadcasted_iota(jnp.int32, (ck, ck), 0)
    col = lax.broadcasted_iota(jnp.int32, (ck, ck), 1)
    causal = col <= row
    tril = _one_hot(causal)
    ones_blk = _one_hot(lane == 0)

    for c in range(tm // ck):
        r0 = c * ck
        gc = gates[r0:r0 + ck]
        logs = jnp.where(lane >= ML_HEADS, jax.nn.log_sigmoid(gc), gc)
        hi = logs.astype(BF16)
        r1 = logs - hi.astype(F32)
        mid = r1.astype(BF16)
        low = (r1 - mid.astype(F32)).astype(BF16)
        cum = _dot(tril, hi) + _dot(tril, mid) + _dot(tril, low)
        col_vals = jnp.where(lane >= ML_HEADS, cum, logs)
        row_vals = col_vals.T
        for hd in range(ML_HEADS):
            q = qk[r0:r0 + ck, hd * ML_DK:(hd + 1) * ML_DK].astype(BF16)
            k32 = qk[r0:r0 + ck, qk_w // 2 + hd * ML_DK:qk_w // 2 + (hd + 1) * ML_DK] * ML_DK ** -0.5
            v = v_all[r0:r0 + ck, hd * ML_DV:(hd + 1) * ML_DV].astype(BF16)
            v_aug = jnp.concatenate([v, ones_blk], axis=1)
            a_col = col_vals[:, ML_HEADS + hd:ML_HEADS + hd + 1]
            li_col = col_vals[:, hd:hd + 1]
            b_row = row_vals[ML_HEADS + hd:ML_HEADS + hd + 1, :]
            li_row = row_vals[hd:hd + 1, :]
            m_in = m_ref[hd]
            c_in = c_ref[hd]

            log_d = jnp.where(causal, a_col - b_row + li_row, -jnp.inf)
            m_intra = jnp.max(log_d, axis=-1, keepdims=True)
            log_inter = a_col + m_in
            m_t = jnp.maximum(log_inter, m_intra)
            inter = jnp.exp(log_inter - m_t)
            p = (_dot_nt(q, k32.astype(BF16)) * jnp.exp(log_d - m_t)).astype(BF16)
            numden = inter * _dot(q, c_in.astype(BF16)) + _dot(p, v_aug)
            den = numden[:, ML_DV:ML_DV + 1]
            h_out = numden[:, :ML_DV] / jnp.maximum(jnp.abs(den), jnp.exp(-m_t))
            h_n = _rms(h_out, hg_ref[:, hd * ML_DV:(hd + 1) * ML_DV])
            gate_o = jax.nn.sigmoid(o_all[r0:r0 + ck, hd * ML_DV:(hd + 1) * ML_DV])
            mix_ref[r0:r0 + ck, hd * ML_DV:(hd + 1) * ML_DV] = (h_n * gate_o).astype(BF16)

            f_tot = b_row[:, ck - 1:ck]
            log_w = f_tot - a_col + li_col
            m_loc = jnp.max(log_w, axis=0, keepdims=True)
            kw_t = (k32 * jnp.exp(log_w - m_loc)).T.astype(BF16)
            c_loc = _dot(kw_t, v_aug)
            m_new = jnp.maximum(f_tot + m_in, m_loc)
            c_ref[hd] = jnp.exp(f_tot + m_in - m_new) * c_in + jnp.exp(m_loc - m_new) * c_loc
            m_ref[hd] = m_new

    mix_ref[:, hv:hv + MEM_WIDTH] = _mem_attention(mem_q, kv_ref).astype(BF16)
    o_ref[...] = x + _dot(mix_ref[...], wout_ref[...])


def _mlstm_layer(h, batch, seq, layer, norm_g, w_in, conv_w, conv_b, gate_b, h_norm, kv_all, w_out,
                 tm=512, ck=256):
    t, d = h.shape
    qk_w = 2 * ML_HEADS * ML_DK
    hv = ML_HEADS * ML_DV
    n_gate = 2 * ML_HEADS
    lo = qk_w + 2 * hv
    w_in_r = jnp.concatenate(
        [w_in[:, :lo], w_in[:, lo + n_gate:], w_in[:, lo:lo + n_gate],
         jnp.zeros((d, LANES - n_gate), w_in.dtype)], axis=1).astype(BF16)
    gb = jnp.pad(gate_b, (0, LANES - n_gate)).reshape(1, LANES)
    n_in = w_in_r.shape[1]
    n_mix = w_out.shape[0]
    tiles = seq // tm
    m, kvw = kv_all.shape[2], kv_all.shape[3]
    return pl.pallas_call(
        functools.partial(_mlstm_layer_kernel, ck=ck),
        grid=(batch, tiles),
        in_specs=[
            pl.BlockSpec((tm, d), lambda b, i: (b * tiles + i, 0)),
            _resident((1, d)),
            _resident((d, n_in)),
            _resident((CONV_K, qk_w)),
            _resident((1, qk_w)),
            _resident((1, LANES)),
            _resident((1, hv)),
            pl.BlockSpec((1, 1, m, kvw), lambda b, i: (layer, b, 0, 0)),
            _resident((n_mix, d)),
        ],
        out_specs=pl.BlockSpec((tm, d), lambda b, i: (b * tiles + i, 0)),
        out_shape=jax.ShapeDtypeStruct((t, d), F32),
        scratch_shapes=[
            pltpu.VMEM((tm + 8, qk_w), F32),
            pltpu.VMEM((ML_HEADS, ML_DK, ML_AUG), F32),
            pltpu.VMEM((ML_HEADS, 1, 1), F32),
            pltpu.VMEM((tm, n_mix), BF16),
        ],
        compiler_params=_params(("arbitrary", "arbitrary")),
        name="mlstm_layer",
    )(h, norm_g.reshape(1, d), w_in_r, conv_w, conv_b.reshape(1, qk_w), gb, h_norm.reshape(1, hv),
      kv_all, w_out.astype(BF16))


def kernel(x, mem, positions, attn_norm, mem_norm, w_mem_kv, ffn_norm, final_norm, gm_w_in, gm_v_norm, gm_w_s, gm_b_s, gm_w_out, mla_w_in, mla_q_norm, mla_w_uq, mla_kv_norm, mla_w_ukv, mla_w_out, ml_w_in, ml_conv_w, ml_conv_b, ml_gate_b, ml_h_norm, ml_w_out, ff_w_gate, ff_w_up, ff_w_down, moe_w_router, moe_b_router, moe_w_gate, moe_w_up, moe_w_down):
    batch, seq, d = x.shape
    depth = attn_norm.shape[0]
    kv_all = _memkv(mem, mem_norm, w_mem_kv)
    ff_w = [w.astype(BF16) for w in (ff_w_gate, ff_w_up, ff_w_down)]
    moe_w = [w.astype(BF16) for w in (moe_w_gate, moe_w_up, moe_w_down)]
    h = x.reshape(batch * seq, d)
    for layer in range(depth):
        kind, j = layer % 3, layer // 3
        c = layer // 2
        last = layer == depth - 1
        if kind == 0:
            h = _gmlp_layer(h, seq, layer, attn_norm[layer], gm_w_in[j], gm_v_norm[j], gm_w_s[j],
                            gm_b_s[j], kv_all, gm_w_out[j])
        elif kind == 1:
            h = _mla_layer(h, batch, seq, layer, positions, attn_norm[layer], mla_w_in[j],
                           mla_q_norm[j], mla_w_uq[j], mla_kv_norm[j], mla_w_ukv[j], kv_all,
                           mla_w_out[j])
        else:
            h = _mlstm_layer(h, batch, seq, layer, attn_norm[layer], ml_w_in[j], ml_conv_w[j],
                             ml_conv_b[j], ml_gate_b[j], ml_h_norm[j], kv_all, ml_w_out[j])
        if layer % 2 == 0:
            h = _ffn_layer(h, ffn_norm[layer], *ff_w, c)
            if last:
                h = _final_norm(h, final_norm)
        else:
            h = _moe_layer(h, ffn_norm[layer], moe_w_router[c], moe_b_router[c], *moe_w, c,
                           final_norm if last else None)
    return h.reshape(batch, seq, d)


def _final_norm_kernel(h_ref, g_ref, o_ref):
    o_ref[...] = _rms(h_ref[...], g_ref[...])


def _final_norm(h, g, tm=1024):
    t, d = h.shape
    return pl.pallas_call(
        _final_norm_kernel,
        grid=(t // tm,),
        in_specs=[pl.BlockSpec((tm, d), lambda i: (i, 0)), _resident((1, d))],
        out_specs=pl.BlockSpec((tm, d), lambda i: (i, 0)),
        out_shape=jax.ShapeDtypeStruct((t, d), F32),
        compiler_params=_params(("arbitrary",)),
        name="final_norm",
    )(h, g.reshape(1, d))
```

```python
import functools

import jax
import jax.numpy as jnp
from jax import lax
from jax.experimental import pallas as pl
from jax.experimental.pallas import tpu as pltpu

F32 = jnp.float32
BF16 = jnp.bfloat16

NORM_EPS = 1e-6
CHUNK = 128
LANES = 128
MEM_HEADS = 4
MEM_HEAD_DIM = 128
MEM_WIDTH = MEM_HEADS * MEM_HEAD_DIM
GM_GROUPS = 8
GM_GROUP_DIM = 128
GM_WIDTH = GM_GROUPS * GM_GROUP_DIM
MLA_HEADS = 8
MLA_NOPE = 128
MLA_ROPE = 64
MLA_V = 128
MLA_Q_LORA = 768
MLA_KV_LORA = 256
MLA_QK_PAD = 256
ROPE_THETA = 10000.0
ML_HEADS = 4
ML_DV = 256
ML_DK = 128
ML_AUG = ML_DV + LANES
CONV_K = 4
N_EXPERTS = 8
TOP_K = 2

VMEM_LIMIT = 56 * 1024 * 1024

NT_DIMS = (((1,), (1,)), ((), ()))


def _params(semantics, vmem=VMEM_LIMIT):
    return pltpu.CompilerParams(dimension_semantics=semantics, vmem_limit_bytes=vmem)


def _resident(shape, index=None):
    index = (0,) * len(shape) if index is None else index
    return pl.BlockSpec(shape, lambda *_: index, pipeline_mode=pl.Buffered(1))


def _dot(a, b):
    return jnp.dot(a, b, preferred_element_type=F32)


def _dot_nt(a, b):
    return lax.dot_general(a, b, NT_DIMS, preferred_element_type=F32)


def _rms(x, g):
    return x * lax.rsqrt(jnp.mean(x * x, axis=-1, keepdims=True) + NORM_EPS) * g


def _silu(x):
    return x * jax.nn.sigmoid(x)


def _gelu(x):
    return 0.5 * x * (1.0 + lax.erf(x * 0.5 ** 0.5))


def _one_hot(mask):
    return jnp.where(mask, 1.0, 0.0).astype(BF16)


def _memkv_kernel(mem_ref, g_ref, w_ref, kv_ref):
    xn = _rms(mem_ref[0], g_ref[0]).astype(BF16)
    kv_ref[0, 0] = _dot(xn, w_ref[0]).astype(BF16)


def _memkv(mem, mem_norm, w_mem_kv):
    b, m, d = mem.shape
    depth = mem_norm.shape[0]
    n = w_mem_kv.shape[-1]
    return pl.pallas_call(
        _memkv_kernel,
        grid=(depth, b),
        in_specs=[
            pl.BlockSpec((1, m, d), lambda l, i: (i, 0, 0)),
            pl.BlockSpec((1, 1, d), lambda l, i: (l, 0, 0)),
            pl.BlockSpec((1, d, n), lambda l, i: (l, 0, 0)),
        ],
        out_specs=pl.BlockSpec((1, 1, m, n), lambda l, i: (l, i, 0, 0)),
        out_shape=jax.ShapeDtypeStruct((depth, b, m, n), BF16),
        compiler_params=_params(("arbitrary", "arbitrary")),
        name="memkv",
    )(mem, mem_norm.reshape(depth, 1, d), w_mem_kv.astype(BF16))


def _mem_attention(q, kv_ref):
    outs = []
    for hd in range(MEM_HEADS):
        lo = hd * MEM_HEAD_DIM
        qh = q[:, lo:lo + MEM_HEAD_DIM].astype(BF16)
        kh = kv_ref[0, 0, :, lo:lo + MEM_HEAD_DIM]
        vh = kv_ref[0, 0, :, MEM_WIDTH + lo:MEM_WIDTH + lo + MEM_HEAD_DIM]
        s = _dot_nt(qh, kh) * MEM_HEAD_DIM ** -0.5
        p = jnp.exp(s - jnp.max(s, axis=-1, keepdims=True))
        outs.append(_dot(p.astype(BF16), vh) / jnp.sum(p, axis=-1, keepdims=True))
    return jnp.concatenate(outs, axis=-1)


def _gmlp_layer_kernel(h_ref, g_ref, win_ref, vg_ref, ws_ref, bs_ref, kv_ref, wout_ref,
                       o_ref, proj_ref, mix_ref):
    tm = h_ref.shape[0]
    x = h_ref[...]
    xn = _rms(x, g_ref[...]).astype(BF16)
    proj_ref[...] = _dot(xn, win_ref[...])
    row = lax.broadcasted_iota(jnp.int32, (CHUNK, CHUNK), 0)
    col = lax.broadcasted_iota(jnp.int32, (CHUNK, CHUNK), 1)
    causal = col <= row
    for g in range(GM_GROUPS):
        lo = g * GM_GROUP_DIM
        u = _gelu(proj_ref[:, lo:lo + GM_GROUP_DIM])
        v = _gelu(proj_ref[:, GM_WIDTH + lo:GM_WIDTH + lo + GM_GROUP_DIM])
        vn = _rms(v, vg_ref[:, lo:lo + GM_GROUP_DIM]).astype(BF16)
        w = jnp.where(causal, ws_ref[g], 0.0).astype(BF16)
        for c in range(tm // CHUNK):
            r = c * CHUNK
            mixed = _dot(w, vn[r:r + CHUNK]) + bs_ref[g]
            mix_ref[r:r + CHUNK, lo:lo + GM_GROUP_DIM] = (u[r:r + CHUNK] * mixed).astype(BF16)
    mem_o = _mem_attention(proj_ref[:, 2 * GM_WIDTH:2 * GM_WIDTH + MEM_WIDTH], kv_ref)
    mix_ref[:, GM_WIDTH:GM_WIDTH + MEM_WIDTH] = mem_o.astype(BF16)
    o_ref[...] = x + _dot(mix_ref[...], wout_ref[...])


def _gmlp_layer(h, seq, layer, norm_g, w_in, v_norm, w_s, b_s, kv_all, w_out, tm=1024):
    t, d = h.shape
    n_in = w_in.shape[1]
    n_mix = w_out.shape[0]
    tiles_per_seq = seq // tm
    m, kvw = kv_all.shape[2], kv_all.shape[3]
    bs_full = jnp.broadcast_to(b_s[:, :, None], (GM_GROUPS, CHUNK, GM_GROUP_DIM)).astype(F32)
    return pl.pallas_call(
        _gmlp_layer_kernel,
        grid=(t // tm,),
        in_specs=[
            pl.BlockSpec((tm, d), lambda i: (i, 0)),
            _resident((1, d)),
            _resident((d, n_in)),
            _resident((1, GM_WIDTH)),
            _resident((GM_GROUPS, CHUNK, CHUNK)),
            _resident((GM_GROUPS, CHUNK, GM_GROUP_DIM)),
            pl.BlockSpec((1, 1, m, kvw), lambda i: (layer, i // tiles_per_seq, 0, 0)),
            _resident((n_mix, d)),
        ],
        out_specs=pl.BlockSpec((tm, d), lambda i: (i, 0)),
        out_shape=jax.ShapeDtypeStruct((t, d), F32),
        scratch_shapes=[pltpu.VMEM((tm, n_in), F32), pltpu.VMEM((tm, n_mix), BF16)],
        compiler_params=_params(("arbitrary",)),
        name="gmlp_layer",
    )(h, norm_g.reshape(1, d), w_in.astype(BF16), v_norm.reshape(1, GM_WIDTH), w_s,
      bs_full, kv_all, w_out.astype(BF16))


FFN_COL_BLOCK = 256


def _ffn_kernel(h_ref, g_ref, wg_ref, wu_ref, wd_ref, o_ref, hid_ref):
    x = h_ref[...]
    xn = _rms(x, g_ref[...]).astype(BF16)
    for lo in range(0, wg_ref.shape[-1], FFN_COL_BLOCK):
        gate = _dot(xn, wg_ref[0, :, lo:lo + FFN_COL_BLOCK])
        up = _dot(xn, wu_ref[0, :, lo:lo + FFN_COL_BLOCK])
        hid_ref[:, lo:lo + FFN_COL_BLOCK] = (_silu(gate) * up).astype(BF16)
    o_ref[...] = x + _dot(hid_ref[...], wd_ref[0])


def _ffn_layer(h, norm_g, w_gate, w_up, w_down, c, tm=512):
    t, d = h.shape
    f = w_gate.shape[-1]
    return pl.pallas_call(
        _ffn_kernel,
        grid=(t // tm,),
        in_specs=[
            pl.BlockSpec((tm, d), lambda i: (i, 0)),
            _resident((1, d)),
            _resident((1, d, f), (c, 0, 0)),
            _resident((1, d, f), (c, 0, 0)),
            _resident((1, f, d), (c, 0, 0)),
        ],
        out_specs=pl.BlockSpec((tm, d), lambda i: (i, 0)),
        out_shape=jax.ShapeDtypeStruct((t, d), F32),
        scratch_shapes=[pltpu.VMEM((tm, f), BF16)],
        compiler_params=_params(("arbitrary",)),
        name="ffn_layer",
    )(h, norm_g.reshape(1, d), w_gate, w_up, w_down)


MOE_SLOT_BLOCK = 256
MOE_SLOT_VARIANTS = (256, 288, 320)
MOE_ROW_WINDOW = 256


def _split_bf16(x):
    hi = x.astype(BF16)
    return hi, (x - hi.astype(F32)).astype(BF16)


def _top2_gates(logits, lane):
    m1 = jnp.max(logits, axis=-1, keepdims=True)
    i1 = jnp.min(jnp.where(logits == m1, lane, LANES), axis=-1, keepdims=True)
    first = lane == i1
    rest = jnp.where(first, -jnp.inf, logits)
    m2 = jnp.max(rest, axis=-1, keepdims=True)
    i2 = jnp.min(jnp.where(rest == m2, lane, LANES), axis=-1, keepdims=True)
    second = lane == i2
    e2 = jnp.exp(m2 - m1)
    denom = 1.0 + e2
    return jnp.where(first, 1.0 / denom, 0.0) + jnp.where(second, e2 / denom, 0.0)


def _moe_kernel(h_ref, g_ref, wr_ref, br_ref, wg_ref, wu_ref, wd_ref, fg_ref,
                o_ref, xn_ref, grow_ref, gcol_ref, rrow_ref, rcol_ref, *, final_norm):
    e = pl.program_id(1)
    tm, d = h_ref.shape

    @pl.when(e == 0)
    def _():
        x = h_ref[...]
        xn = _rms(x, g_ref[...])
        o_ref[...] = x
        x_hi, x_lo = _split_bf16(xn)
        w_hi, w_lo = _split_bf16(wr_ref[...])
        xn_ref[...] = x_hi
        hi_terms = _dot(x_hi, jnp.concatenate([w_hi, w_lo], axis=1))
        logits = hi_terms[:, :LANES] + hi_terms[:, LANES:] + _dot(x_lo, w_hi) + br_ref[...]
        lane = lax.broadcasted_iota(jnp.int32, (tm, LANES), 1)
        gcol = _top2_gates(jnp.where(lane < N_EXPERTS, logits, -jnp.inf), lane)
        gcol_ref[...] = gcol
        grow_ref[...] = gcol.T[:N_EXPERTS]
        sel = _one_hot(gcol != 0.0)
        r = lax.broadcasted_iota(jnp.int32, (CHUNK, CHUNK), 0)
        c = lax.broadcasted_iota(jnp.int32, (CHUNK, CHUNK), 1)
        earlier = _one_hot(c < r)
        base = jnp.zeros((1, LANES), F32)
        for b in range(tm // CHUNK):
            blk = sel[b * CHUNK:(b + 1) * CHUNK]
            rcol_ref[b * CHUNK:(b + 1) * CHUNK, :] = _dot(earlier, blk) + base
            base = base + jnp.sum(blk.astype(F32), axis=0, keepdims=True)
        rrow_ref[...] = rcol_ref[...].T[:N_EXPERTS]

    gate_row = grow_ref[pl.ds(e, 1), :]
    rank_row = rrow_ref[pl.ds(e, 1), :]
    picked = jnp.where(gate_row != 0.0, 1.0, 0.0)
    token = lax.broadcasted_iota(jnp.int32, (1, tm), 1)
    count_upto = [jnp.sum(jnp.where(token < (q + 1) * MOE_ROW_WINDOW, picked, 0.0)).astype(jnp.int32)
                  for q in range(tm // MOE_ROW_WINDOW)]
    count = count_upto[-1]

    def expert_block(cs, first_slot):
        lane = lax.broadcasted_iota(jnp.int32, (tm, LANES), 1)
        gate_col = jnp.sum(jnp.where(lane == e, gcol_ref[...], 0.0), axis=-1, keepdims=True)
        rank_col = jnp.sum(jnp.where(lane == e, rcol_ref[...], 0.0), axis=-1, keepdims=True)
        off = jnp.full((1, 1), first_slot, jnp.int32).astype(F32)
        slot_s = lax.broadcasted_iota(jnp.int32, (cs, tm), 0).astype(F32)
        gather = _one_hot(jnp.logical_and(rank_row - off == slot_s, gate_row != 0.0))
        xg = _dot(gather, xn_ref[...]).astype(BF16)
        hid = (_silu(_dot(xg, wg_ref[0, 0])) * _dot(xg, wu_ref[0, 0])).astype(BF16)
        y = _dot(hid, wd_ref[0, 0]).astype(BF16)
        main = MOE_SLOT_BLOCK
        slot_l = lax.broadcasted_iota(jnp.int32, (tm, main), 1).astype(F32)
        scatter = _one_hot(jnp.logical_and(rank_col - off == slot_l, gate_col != 0.0))
        o_ref[...] += gate_col * _dot(scatter, y[:main])
        if cs > main:
            y_over = jnp.concatenate([y[main:], jnp.zeros((LANES - (cs - main), d), BF16)], axis=0)
            slot_o = lax.broadcasted_iota(jnp.int32, (MOE_ROW_WINDOW, LANES), 1).astype(F32) + main
            for q in range(tm // MOE_ROW_WINDOW):
                @pl.when(count_upto[q] > main)
                def _(q=q):
                    rows = slice(q * MOE_ROW_WINDOW, (q + 1) * MOE_ROW_WINDOW)
                    put = jnp.logical_and(rank_col[rows] - off == slot_o, gate_col[rows] != 0.0)
                    o_ref[rows, :] += gate_col[rows] * _dot(_one_hot(put), y_over)

    lower = 0
    for cs in MOE_SLOT_VARIANTS:
        @pl.when(jnp.logical_and(count > lower, count <= cs))
        def _(cs=cs):
            expert_block(cs, 0)
        lower = cs

    @pl.when(count > lower)
    def _():
        def body(b, carry):
            expert_block(MOE_SLOT_BLOCK, b * MOE_SLOT_BLOCK)
            return carry
        lax.fori_loop(0, (count + MOE_SLOT_BLOCK - 1) // MOE_SLOT_BLOCK, body, 0)

    if final_norm:
        @pl.when(e == pl.num_programs(1) - 1)
        def _():
            o_ref[...] = _rms(o_ref[...], fg_ref[...])


def _moe_layer(h, norm_g, w_router, b_router, w_gate, w_up, w_down, c, final_g, tm=1024):
    t, d = h.shape
    f = w_gate.shape[-1]
    final_norm = final_g is not None
    fg = (final_g if final_norm else jnp.ones((d,), F32)).reshape(1, d)
    wr = jnp.pad(w_router, ((0, 0), (0, LANES - N_EXPERTS)))
    br = jnp.pad(b_router, (0, LANES - N_EXPERTS)).reshape(1, LANES)
    return pl.pallas_call(
        functools.partial(_moe_kernel, final_norm=final_norm),
        grid=(t // tm, N_EXPERTS),
        in_specs=[
            pl.BlockSpec((tm, d), lambda i, e: (i, 0)),
            _resident((1, d)),
            _resident((d, LANES)),
            _resident((1, LANES)),
            pl.BlockSpec((1, 1, d, f), lambda i, e: (c, e, 0, 0)),
            pl.BlockSpec((1, 1, d, f), lambda i, e: (c, e, 0, 0)),
            pl.BlockSpec((1, 1, f, d), lambda i, e: (c, e, 0, 0)),
            _resident((1, d)),
        ],
        out_specs=pl.BlockSpec((tm, d), lambda i, e: (i, 0)),
        out_shape=jax.ShapeDtypeStruct((t, d), F32),
        scratch_shapes=[
            pltpu.VMEM((tm, d), BF16),
            pltpu.VMEM((N_EXPERTS, tm), F32),
            pltpu.VMEM((tm, LANES), F32),
            pltpu.VMEM((N_EXPERTS, tm), F32),
            pltpu.VMEM((tm, LANES), F32),
        ],
        compiler_params=_params(("arbitrary", "arbitrary")),
        name="moe_layer",
    )(h, norm_g.reshape(1, d), wr, br, w_gate, w_up, w_down, fg)


def _mla_proj_kernel(h_ref, g_ref, pos_ref, freq_ref, win_ref, qg_ref, wuq_ref, kvg_ref, wukv_ref,
                     kv_ref, q_out, k_out, v_out, memo_out):
    xn = _rms(h_ref[...], g_ref[...]).astype(BF16)
    proj = _dot(xn, win_ref[...])
    c_q = proj[:, :MLA_Q_LORA]
    c_kv = proj[:, MLA_Q_LORA:MLA_Q_LORA + MLA_KV_LORA]
    lo = MLA_Q_LORA + MLA_KV_LORA
    mem_q = proj[:, lo:lo + MEM_WIDTH]
    k_rope = proj[:, lo + MEM_WIDTH:lo + MEM_WIDTH + LANES]
    memo_out[...] = _mem_attention(mem_q, kv_ref).astype(BF16)

    ang = pos_ref[...].astype(F32) * freq_ref[...]
    lane = lax.broadcasted_iota(jnp.int32, ang.shape, 1)
    cos = jnp.where(lane < MLA_ROPE, jnp.cos(ang), 0.0)
    sin = jnp.sin(ang)
    sin = jnp.where(lane < MLA_ROPE // 2, -sin, jnp.where(lane < MLA_ROPE, sin, 0.0))

    def rope(r):
        return r * cos + pltpu.roll(r, MLA_ROPE // 2, 1) * sin

    scale = (MLA_NOPE + MLA_ROPE) ** -0.5
    q = _dot(_rms(c_q, qg_ref[...]).astype(BF16), wuq_ref[...])
    kvu = _dot(_rms(c_kv, kvg_ref[...]).astype(BF16), wukv_ref[...])
    k_r = rope(k_rope).astype(BF16)
    for hd in range(MLA_HEADS):
        qlo = hd * MLA_QK_PAD
        q_out[:, qlo:qlo + MLA_NOPE] = (q[:, qlo:qlo + MLA_NOPE] * scale).astype(BF16)
        q_out[:, qlo + MLA_NOPE:qlo + MLA_QK_PAD] = (
            rope(q[:, qlo + MLA_NOPE:qlo + MLA_QK_PAD]) * scale).astype(BF16)
        k_out[:, qlo:qlo + MLA_NOPE] = kvu[:, hd * MLA_NOPE:(hd + 1) * MLA_NOPE].astype(BF16)
        k_out[:, qlo + MLA_NOPE:qlo + MLA_QK_PAD] = k_r
    v_out[...] = kvu[:, MLA_HEADS * MLA_NOPE:].astype(BF16)


def _flash_kernel(qi_ref, kj_ref, q_ref, k_ref, v_ref, h_ref, memo_ref, wa_ref, wm_ref, o_ref,
                  m_ref, acc_ref, attn_ref):
    step_id = pl.program_id(1)
    i = qi_ref[step_id]
    j = kj_ref[step_id]
    tq, tk = q_ref.shape[0], k_ref.shape[0]

    @pl.when(j == 0)
    def _():
        m_ref[...] = jnp.full_like(m_ref, -jnp.inf)
        acc_ref[...] = jnp.zeros_like(acc_ref)

    ones = jnp.ones((tk, MLA_V), BF16)

    def step(masked):
        if masked:
            row = lax.broadcasted_iota(jnp.int32, (tq, tk), 0)
            col = lax.broadcasted_iota(jnp.int32, (tq, tk), 1)
            keep = col <= row
        for hd in range(MLA_HEADS):
            q = q_ref[:, hd * MLA_QK_PAD:(hd + 1) * MLA_QK_PAD]
            k = k_ref[:, hd * MLA_QK_PAD:(hd + 1) * MLA_QK_PAD]
            v_aug = jnp.concatenate([v_ref[:, hd * MLA_V:(hd + 1) * MLA_V], ones], axis=1)
            s = _dot_nt(q, k)
            if masked:
                s = jnp.where(keep, s, -jnp.inf)
            m_prev = m_ref[hd]
            m_new = jnp.maximum(m_prev, jnp.max(s, axis=-1, keepdims=True))
            alpha = jnp.exp(m_prev - m_new)
            p = jnp.exp((s - jnp.tile(m_new, (1, tk // LANES))).astype(BF16))
            acc_ref[hd] = jnp.tile(alpha, (1, 2)) * acc_ref[hd] + _dot(p, v_aug)
            m_ref[hd] = m_new

    @pl.when(j < i)
    def _():
        step(False)

    @pl.when(j == i)
    def _():
        step(True)
        for hd in range(MLA_HEADS):
            acc = acc_ref[hd]
            attn_ref[:, hd * MLA_V:(hd + 1) * MLA_V] = (acc[:, :MLA_V] / acc[:, MLA_V:]).astype(BF16)
        o_ref[...] = (h_ref[...] + _dot(attn_ref[...], wa_ref[...])
                      + _dot(memo_ref[...], wm_ref[...]))


def _mla_layer(h, batch, seq, layer, positions, norm_g, w_in, q_norm, w_uq, kv_norm, w_ukv,
               kv_all, w_out, tm=1024, tq=512):
    t, d = h.shape
    half = MLA_ROPE // 2
    lo = MLA_Q_LORA + MLA_KV_LORA
    w_kr = w_in[:, lo:lo + MLA_ROPE]
    w_in_r = jnp.concatenate([w_in[:, :lo], w_in[:, lo + MLA_ROPE:], w_kr, w_kr], axis=1).astype(BF16)
    wq = w_uq.reshape(MLA_Q_LORA, MLA_HEADS, MLA_NOPE + MLA_ROPE)
    wq_r = jnp.concatenate([wq, wq[:, :, MLA_NOPE:]], axis=2).reshape(
        MLA_Q_LORA, MLA_HEADS * MLA_QK_PAD).astype(BF16)
    wkv = w_ukv.reshape(MLA_KV_LORA, MLA_HEADS, MLA_NOPE + MLA_V)
    wkv_r = jnp.concatenate([wkv[:, :, :MLA_NOPE].reshape(MLA_KV_LORA, -1),
                             wkv[:, :, MLA_NOPE:].reshape(MLA_KV_LORA, -1)], axis=1).astype(BF16)
    inv_freq = 1.0 / (ROPE_THETA ** (jnp.arange(0, MLA_ROPE, 2, dtype=F32) / MLA_ROPE))
    freq = jnp.tile(inv_freq, LANES // half).reshape(1, LANES)
    n_in = w_in_r.shape[1]
    tiles_per_seq = seq // tm
    m, kvw = kv_all.shape[2], kv_all.shape[3]
    qk_w = MLA_HEADS * MLA_QK_PAD
    v_w = MLA_HEADS * MLA_V
    q, k, v, memo = pl.pallas_call(
        _mla_proj_kernel,
        grid=(t // tm,),
        in_specs=[
            pl.BlockSpec((tm, d), lambda i: (i, 0)),
            _resident((1, d)),
            pl.BlockSpec((tm, 1), lambda i: (i, 0)),
            _resident((1, LANES)),
            _resident((d, n_in)),
            _resident((1, MLA_Q_LORA)),
            _resident((MLA_Q_LORA, qk_w)),
            _resident((1, MLA_KV_LORA)),
            _resident((MLA_KV_LORA, 2 * v_w)),
            pl.BlockSpec((1, 1, m, kvw), lambda i: (layer, i // tiles_per_seq, 0, 0)),
        ],
        out_specs=[
            pl.BlockSpec((tm, qk_w), lambda i: (i, 0)),
            pl.BlockSpec((tm, qk_w), lambda i: (i, 0)),
            pl.BlockSpec((tm, v_w), lambda i: (i, 0)),
            pl.BlockSpec((tm, MEM_WIDTH), lambda i: (i, 0)),
        ],
        out_shape=[
            jax.ShapeDtypeStruct((t, qk_w), BF16),
            jax.ShapeDtypeStruct((t, qk_w), BF16),
            jax.ShapeDtypeStruct((t, v_w), BF16),
            jax.ShapeDtypeStruct((t, MEM_WIDTH), BF16),
        ],
        compiler_params=_params(("arbitrary",)),
        name="mla_proj",
    )(h, norm_g.reshape(1, d), positions.reshape(t, 1), freq, w_in_r, q_norm.reshape(1, -1), wq_r,
      kv_norm.reshape(1, -1), wkv_r, kv_all)

    nq = seq // tq
    pairs = [(i, j) for i in range(nq) for j in range(i + 1)]
    qi = jnp.array([p[0] for p in pairs], jnp.int32)
    kj = jnp.array([p[1] for p in pairs], jnp.int32)
    w_out_b = w_out.astype(BF16)
    q_tile = lambda b, s, qi, kj: (b * nq + qi[s], 0)
    k_tile = lambda b, s, qi, kj: (b * nq + kj[s], 0)
    return pl.pallas_call(
        _flash_kernel,
        grid_spec=pltpu.PrefetchScalarGridSpec(
            num_scalar_prefetch=2,
            grid=(batch, len(pairs)),
            in_specs=[
                pl.BlockSpec((tq, qk_w), q_tile),
                pl.BlockSpec((tq, qk_w), k_tile),
                pl.BlockSpec((tq, v_w), k_tile),
                pl.BlockSpec((tq, d), q_tile),
                pl.BlockSpec((tq, MEM_WIDTH), q_tile),
                _resident((v_w, d)),
                _resident((MEM_WIDTH, d)),
            ],
            out_specs=pl.BlockSpec((tq, d), q_tile),
            scratch_shapes=[
                pltpu.VMEM((MLA_HEADS, tq, LANES), F32),
                pltpu.VMEM((MLA_HEADS, tq, 2 * MLA_V), F32),
                pltpu.VMEM((tq, v_w), BF16),
            ],
        ),
        out_shape=jax.ShapeDtypeStruct((t, d), F32),
        compiler_params=_params(("arbitrary", "arbitrary")),
        name="mla_flash",
    )(qi, kj, q, k, v, h, memo, w_out_b[:v_w], w_out_b[v_w:])


def _mlstm_layer_kernel(h_ref, g_ref, win_ref, cw_ref, cb_ref, gb_ref, hg_ref, kv_ref, wout_ref,
                        o_ref, qk_ref, c_ref, m_ref, mix_ref, *, ck):
    tm = h_ref.shape[0]
    qk_w = 2 * ML_HEADS * ML_DK
    hv = ML_HEADS * ML_DV
    pad = 8

    @pl.when(pl.program_id(1) == 0)
    def _():
        qk_ref[0:pad, :] = jnp.zeros((pad, qk_w), F32)
        c_ref[...] = jnp.zeros_like(c_ref)
        m_ref[...] = jnp.zeros_like(m_ref)

    x = h_ref[...]
    xn = _rms(x, g_ref[...]).astype(BF16)
    proj = _dot(xn, win_ref[...])
    qk_ref[pad:pad + tm, :] = proj[:, :qk_w]
    conv = cb_ref[...]
    for j in range(CONV_K):
        conv = conv + cw_ref[j:j + 1, :] * qk_ref[pad - (CONV_K - 1) + j:pad - (CONV_K - 1) + j + tm, :]
    qk_ref[0:pad, :] = qk_ref[tm:tm + pad, :]
    qk = _silu(conv)
    v_all = proj[:, qk_w:qk_w + hv]
    o_all = proj[:, qk_w + hv:qk_w + 2 * hv]
    mem_q = proj[:, qk_w + 2 * hv:qk_w + 2 * hv + MEM_WIDTH]
    gates = proj[:, qk_w + 2 * hv + MEM_WIDTH:] + gb_ref[...]

    lane = lax.broadcasted_iota(jnp.int32, (ck, LANES), 1)
    row = lax.broadcasted_iota(jnp.int32, (ck, ck), 0)
    col = lax.broadcasted_iota(jnp.int32, (ck, ck), 1)
    causal = col <= row
    tril = _one_hot(causal)
    ones_blk = _one_hot(lane == 0)

    for c in range(tm // ck):
        r0 = c * ck
        gc = gates[r0:r0 + ck]
        logs = jnp.where(lane >= ML_HEADS, jax.nn.log_sigmoid(gc), gc)
        hi = logs.astype(BF16)
        r1 = logs - hi.astype(F32)
        mid = r1.astype(BF16)
        low = (r1 - mid.astype(F32)).astype(BF16)
        cum = _dot(tril, hi) + _dot(tril, mid) + _dot(tril, low)
        col_vals = jnp.where(lane >= ML_HEADS, cum, logs)
        row_vals = col_vals.T
        for hd in range(ML_HEADS):
            q = qk[r0:r0 + ck, hd * ML_DK:(hd + 1) * ML_DK].astype(BF16)
            k32 = qk[r0:r0 + ck, qk_w // 2 + hd * ML_DK:qk_w // 2 + (hd + 1) * ML_DK] * ML_DK ** -0.5
            v = v_all[r0:r0 + ck, hd * ML_DV:(hd + 1) * ML_DV].astype(BF16)
            v_aug = jnp.concatenate([v, ones_blk], axis=1)
            a_col = col_vals[:, ML_HEADS + hd:ML_HEADS + hd + 1]
            li_col = col_vals[:, hd:hd + 1]
            b_row = row_vals[ML_HEADS + hd:ML_HEADS + hd + 1, :]
            li_row = row_vals[hd:hd + 1, :]
            m_in = m_ref[hd]
            c_in = c_ref[hd]

            log_d = jnp.where(causal, a_col - b_row + li_row, -jnp.inf)
            m_intra = jnp.max(log_d, axis=-1, keepdims=True)
            log_inter = a_col + m_in
            m_t = jnp.maximum(log_inter, m_intra)
            inter = jnp.exp(log_inter - m_t)
            p = (_dot_nt(q, k32.astype(BF16)) * jnp.exp(log_d - m_t)).astype(BF16)
            numden = inter * _dot(q, c_in.astype(BF16)) + _dot(p, v_aug)
            den = numden[:, ML_DV:ML_DV + 1]
            h_out = numden[:, :ML_DV] / jnp.maximum(jnp.abs(den), jnp.exp(-m_t))
            h_n = _rms(h_out, hg_ref[:, hd * ML_DV:(hd + 1) * ML_DV])
            gate_o = jax.nn.sigmoid(o_all[r0:r0 + ck, hd * ML_DV:(hd + 1) * ML_DV])
            mix_ref[r0:r0 + ck, hd * ML_DV:(hd + 1) * ML_DV] = (h_n * gate_o).astype(BF16)

            f_tot = b_row[:, ck - 1:ck]
            log_w = f_tot - a_col + li_col
            m_loc = jnp.max(log_w, axis=0, keepdims=True)
            kw_t = (k32 * jnp.exp(log_w - m_loc)).T.astype(BF16)
            c_loc = _dot(kw_t, v_aug)
            m_new = jnp.maximum(f_tot + m_in, m_loc)
            c_ref[hd] = jnp.exp(f_tot + m_in - m_new) * c_in + jnp.exp(m_loc - m_new) * c_loc
            m_ref[hd] = m_new

    mix_ref[:, hv:hv + MEM_WIDTH] = _mem_attention(mem_q, kv_ref).astype(BF16)
    o_ref[...] = x + _dot(mix_ref[...], wout_ref[...])


def _mlstm_layer(h, batch, seq, layer, norm_g, w_in, conv_w, conv_b, gate_b, h_norm, kv_all, w_out,
                 tm=512, ck=256):
    t, d = h.shape
    qk_w = 2 * ML_HEADS * ML_DK
    hv = ML_HEADS * ML_DV
    n_gate = 2 * ML_HEADS
    lo = qk_w + 2 * hv
    w_in_r = jnp.concatenate(
        [w_in[:, :lo], w_in[:, lo + n_gate:], w_in[:, lo:lo + n_gate],
         jnp.zeros((d, LANES - n_gate), w_in.dtype)], axis=1).astype(BF16)
    gb = jnp.pad(gate_b, (0, LANES - n_gate)).reshape(1, LANES)
    n_in = w_in_r.shape[1]
    n_mix = w_out.shape[0]
    tiles = seq // tm
    m, kvw = kv_all.shape[2], kv_all.shape[3]
    return pl.pallas_call(
        functools.partial(_mlstm_layer_kernel, ck=ck),
        grid=(batch, tiles),
        in_specs=[
            pl.BlockSpec((tm, d), lambda b, i: (b * tiles + i, 0)),
            _resident((1, d)),
            _resident((d, n_in)),
            _resident((CONV_K, qk_w)),
            _resident((1, qk_w)),
            _resident((1, LANES)),
            _resident((1, hv)),
            pl.BlockSpec((1, 1, m, kvw), lambda b, i: (layer, b, 0, 0)),
            _resident((n_mix, d)),
        ],
        out_specs=pl.BlockSpec((tm, d), lambda b, i: (b * tiles + i, 0)),
        out_shape=jax.ShapeDtypeStruct((t, d), F32),
        scratch_shapes=[
            pltpu.VMEM((tm + 8, qk_w), F32),
            pltpu.VMEM((ML_HEADS, ML_DK, ML_AUG), F32),
            pltpu.VMEM((ML_HEADS, 1, 1), F32),
            pltpu.VMEM((tm, n_mix), BF16),
        ],
        compiler_params=_params(("arbitrary", "arbitrary")),
        name="mlstm_layer",
    )(h, norm_g.reshape(1, d), w_in_r, conv_w, conv_b.reshape(1, qk_w), gb, h_norm.reshape(1, hv),
      kv_all, w_out.astype(BF16))


def kernel(x, mem, positions, attn_norm, mem_norm, w_mem_kv, ffn_norm, final_norm, gm_w_in, gm_v_norm, gm_w_s, gm_b_s, gm_w_out, mla_w_in, mla_q_norm, mla_w_uq, mla_kv_norm, mla_w_ukv, mla_w_out, ml_w_in, ml_conv_w, ml_conv_b, ml_gate_b, ml_h_norm, ml_w_out, ff_w_gate, ff_w_up, ff_w_down, moe_w_router, moe_b_router, moe_w_gate, moe_w_up, moe_w_down):
    batch, seq, d = x.shape
    depth = attn_norm.shape[0]
    kv_all = _memkv(mem, mem_norm, w_mem_kv)
    ff_w = [w.astype(BF16) for w in (ff_w_gate, ff_w_up, ff_w_down)]
    moe_w = [w.astype(BF16) for w in (moe_w_gate, moe_w_up, moe_w_down)]
    h = x.reshape(batch * seq, d)
    for layer in range(depth):
        kind, j = layer % 3, layer // 3
        c = layer // 2
        last = layer == depth - 1
        if kind == 0:
            h = _gmlp_layer(h, seq, layer, attn_norm[layer], gm_w_in[j], gm_v_norm[j], gm_w_s[j],
                            gm_b_s[j], kv_all, gm_w_out[j])
        elif kind == 1:
            h = _mla_layer(h, batch, seq, layer, positions, attn_norm[layer], mla_w_in[j],
                           mla_q_norm[j], mla_w_uq[j], mla_kv_norm[j], mla_w_ukv[j], kv_all,
                           mla_w_out[j])
        else:
            h = _mlstm_layer(h, batch, seq, layer, attn_norm[layer], ml_w_in[j], ml_conv_w[j],
                             ml_conv_b[j], ml_gate_b[j], ml_h_norm[j], kv_all, ml_w_out[j])
        if layer % 2 == 0:
            h = _ffn_layer(h, ffn_norm[layer], *ff_w, c)
            if last:
                h = _final_norm(h, final_norm)
        else:
            h = _moe_layer(h, ffn_norm[layer], moe_w_router[c], moe_b_router[c], *moe_w, c,
                           final_norm if last else None)
    return h.reshape(batch, seq, d)


def _final_norm_kernel(h_ref, g_ref, o_ref):
    o_ref[...] = _rms(h_ref[...], g_ref[...])


def _final_norm(h, g, tm=1024):
    t, d = h.shape
    return pl.pallas_call(
        _final_norm_kernel,
        grid=(t // tm,),
        in_specs=[pl.BlockSpec((tm, d), lambda i: (i, 0)), _resident((1, d))],
        out_specs=pl.BlockSpec((tm, d), lambda i: (i, 0)),
        out_shape=jax.ShapeDtypeStruct((t, d), F32),
        compiler_params=_params(("arbitrary",)),
        name="final_norm",
    )(h, g.reshape(1, d))
```
